```python
import math
import jax, jax.numpy as jnp
from jax import lax
import numpy as np

D_MODEL = 4096
BATCH = 1
SEQ = 8192
DEPTH = 1

NH_M = 4
DQK_M = 256
DV_M = 512
CONV_W = 4
NH_R = 4
DQK_R = 256
DV_R = 512
D_FF = 11008
CHUNK = 128
ROPE_BASE = 10000.0
LN_EPS = 1e-5
HEAD_EPS = 1e-6
ALPHA = (2.0 * DEPTH) ** 0.25
BETA = (8.0 * DEPTH) ** -0.25

WM_QK = NH_M * DQK_M
WM_V = NH_M * DV_M
WR_QK = NH_R * DQK_R
WR_V = NH_R * DV_R
D_IN = 2 * WM_QK + 2 * WM_V + 2 * NH_M + 2 * WR_QK + 2 * WR_V + 2 * D_MODEL

kernel_name = 'hybrid_mlstm_retention_macaron_deepnorm'


def layer_norm(x, w, b):
    xf = x.astype(jnp.float32)
    mu = xf.mean(-1, keepdims=True)
    var = jnp.square(xf - mu).mean(-1, keepdims=True)
    return ((xf - mu) * lax.rsqrt(var + LN_EPS) * w + b).astype(x.dtype)


def head_norm(h, w):
    B, S, H, Dh = h.shape
    hf = h.astype(jnp.float32)
    mu = hf.mean(-1, keepdims=True)
    var = jnp.square(hf - mu).mean(-1, keepdims=True)
    out = (hf - mu) * lax.rsqrt(var + HEAD_EPS) * w.reshape(H, Dh)
    return out.reshape(B, S, H * Dh)


def swiglu(x, w_gate, w_up, w_down):
    return (jax.nn.silu(x @ w_gate) * (x @ w_up)) @ w_down


def causal_dwconv(x, w, b):
    C = x.shape[-1]
    out = lax.conv_general_dilated(x, w[:, None, :].astype(x.dtype), window_strides=(1,),
                                   padding=((CONV_W - 1, 0),),
                                   dimension_numbers=('NWC', 'WIO', 'NWC'),
                                   feature_group_count=C)
    return out + b


def rope(x, positions):
    Dh = x.shape[-1]
    inv = ROPE_BASE ** (-jnp.arange(0, Dh, 2, dtype=jnp.float32) / Dh)
    ang = positions.astype(jnp.float32)[..., None] * inv
    cos = jnp.cos(ang)[:, :, None, :]
    sin = jnp.sin(ang)[:, :, None, :]
    x1, x2 = x[..., :Dh // 2], x[..., Dh // 2:]
    return jnp.concatenate([x1 * cos - x2 * sin, x2 * cos + x1 * sin], axis=-1)


def to_chunks(x):
    B, S, H, D = x.shape
    return x.reshape(B, S // CHUNK, CHUNK, H, D).transpose(1, 0, 3, 2, 4)


def from_chunks(y):
    N, B, H, L, D = y.shape
    return y.transpose(1, 0, 3, 2, 4).reshape(B, N * L, H, D)


def gate_chunks(g):
    B, S, H = g.shape
    return g.reshape(B, S // CHUNK, CHUNK, H).transpose(1, 0, 3, 2)


def mlstm_chunkwise(q, k, v, i_pre, f_pre):
    B, S, H, DK = q.shape
    DV = v.shape[-1]
    qc = to_chunks(q.astype(jnp.float32)) * (DK ** -0.5)
    kc = to_chunks(k.astype(jnp.float32))
    vc = to_chunks(v.astype(jnp.float32))
    ic = gate_chunks(i_pre.astype(jnp.float32))
    lfc = gate_chunks(jax.nn.log_sigmoid(f_pre.astype(jnp.float32)))
    causal = jnp.tril(jnp.ones((CHUNK, CHUNK), dtype=bool))

    def step(carry, inp):
        C, n, m = carry
        qb, kb, vb, ib, lfb = inp
        b = jnp.cumsum(lfb, axis=-1)
        D = b[..., :, None] - b[..., None, :] + ib[..., None, :]
        D = jnp.where(causal, D, -jnp.inf)
        inter = b + m[..., None]
        m_t = jnp.maximum(inter, D.max(-1))
        w_intra = jnp.exp(D - m_t[..., None])
        w_inter = jnp.exp(inter - m_t)
        s = jnp.einsum('bhld,bhsd->bhls', qb, kb) * w_intra
        num = jnp.einsum('bhls,bhsv->bhlv', s, vb) + w_inter[..., None] * jnp.einsum('bhld,bhdv->bhlv', qb, C)
        den = s.sum(-1) + w_inter * jnp.einsum('bhld,bhd->bhl', qb, n)
        h = num / jnp.maximum(jnp.abs(den), jnp.exp(-m_t))[..., None]
        m_new = m_t[..., -1]
        w_k = jnp.exp(b[..., -1:] - b + ib - m_new[..., None])
        decay = jnp.exp(b[..., -1] + m - m_new)
        C_new = decay[..., None, None] * C + jnp.einsum('bhsd,bhsv->bhdv', kb * w_k[..., None], vb)
        n_new = decay[..., None] * n + jnp.einsum('bhsd,bhs->bhd', kb, w_k)
        return (C_new, n_new, m_new), h

    init = (jnp.zeros((B, H, DK, DV), jnp.float32),
            jnp.zeros((B, H, DK), jnp.float32),
            jnp.zeros((B, H), jnp.float32))
    _, hs = lax.scan(step, init, (qc, kc, vc, ic, lfc))
    return from_chunks(hs)


def retention_chunkwise(q, k, v):
    B, S, H, DK = q.shape
    DV = v.shape[-1]
    log_g = jnp.log(1.0 - 2.0 ** (-5.0 - jnp.arange(H, dtype=jnp.float32)))
    pos = jnp.arange(CHUNK, dtype=jnp.float32)
    diff = pos[:, None] - pos[None, :]
    decay_mask = jnp.where(diff >= 0, jnp.exp(log_g[:, None, None] * jnp.maximum(diff, 0.0)), 0.0)
    q_decay = jnp.exp(log_g[:, None] * (pos + 1.0))
    k_decay = jnp.exp(log_g[:, None] * (CHUNK - 1.0 - pos))
    chunk_decay = jnp.exp(log_g * CHUNK)
    qc = to_chunks(q.astype(jnp.float32))
    kc = to_chunks(k.astype(jnp.float32)) * (DK ** -0.5)
    vc = to_chunks(v.astype(jnp.float32))

    def step(R, inp):
        qb, kb, vb = inp
        s = jnp.einsum('bhld,bhsd->bhls', qb, kb) * decay_mask
        intra = jnp.einsum('bhls,bhsv->bhlv', s, vb)
        inter = jnp.einsum('bhld,bhdv->bhlv', qb * q_decay[..., None], R)
        R_new = chunk_decay[:, None, None] * R + jnp.einsum('bhsd,bhsv->bhdv', kb * k_decay[..., None], vb)
        return R_new, intra + inter

    _, ys = lax.scan(step, jnp.zeros((B, H, DK, DV), jnp.float32), (qc, kc, vc))
    return from_chunks(ys)


def token_mixer(x, positions, w_in, if_bias, conv_w, conv_b, norm_m_w, norm_r_w, w_proj_m, w_proj_r, w_out):
    B, S, _ = x.shape
    proj = x @ w_in
    sizes = (WM_QK, WM_QK, WM_V, WM_V, 2 * NH_M, WR_QK, WR_QK, WR_V, WR_V, D_MODEL, D_MODEL)
    cuts = tuple(int(c) for c in np.cumsum(sizes)[:-1])
    q_m, k_m, v_m, o_m, if_m, q_r, k_r, v_r, g_r, gate_a, gate_b = jnp.split(proj, cuts, axis=-1)

    qk_m = jax.nn.silu(causal_dwconv(jnp.concatenate([q_m, k_m], axis=-1), conv_w, conv_b))
    q_m, k_m = qk_m[..., :WM_QK], qk_m[..., WM_QK:]
    gates = if_m.astype(jnp.float32) + if_bias
    h_m = mlstm_chunkwise(q_m.reshape(B, S, NH_M, DQK_M), k_m.reshape(B, S, NH_M, DQK_M),
                          v_m.reshape(B, S, NH_M, DV_M), gates[..., :NH_M], gates[..., NH_M:])
    y_m = (head_norm(h_m, norm_m_w) * jax.nn.sigmoid(o_m.astype(jnp.float32))).astype(x.dtype)

    q_r = rope(q_r.reshape(B, S, NH_R, DQK_R), positions)
    k_r = rope(k_r.reshape(B, S, NH_R, DQK_R), positions)
    h_r = retention_chunkwise(q_r, k_r, v_r.reshape(B, S, NH_R, DV_R))
    y_r = (head_norm(h_r, norm_r_w) * jax.nn.silu(g_r.astype(jnp.float32))).astype(x.dtype)

    merged = jax.nn.sigmoid(gate_a) * (y_m @ w_proj_m) + jax.nn.sigmoid(gate_b) * (y_r @ w_proj_r)
    return merged @ w_out


def setup_inputs(seed: int = 0) -> dict:
    key = jax.random.key(seed)
    ks = jax.random.split(key, 24)
    f32 = jnp.float32
    nrm = lambda k, shape, scale: jax.random.normal(k, shape, f32) * scale
    L = DEPTH
    f_bias = jnp.linspace(3.0, 6.0, NH_M, dtype=f32)[None, :] + nrm(ks[9], (L, NH_M), 0.1)
    i_bias = nrm(ks[8], (L, NH_M), 0.1)
    return {
        'x': nrm(ks[0], (BATCH, SEQ, D_MODEL), 1.0),
        'positions': jnp.broadcast_to(jnp.arange(SEQ, dtype=jnp.int32), (BATCH, SEQ)),
        'ffn1_w_gate': nrm(ks[1], (L, D_MODEL, D_FF), D_MODEL ** -0.5),
        'ffn1_w_up': nrm(ks[2], (L, D_MODEL, D_FF), D_MODEL ** -0.5),
        'ffn1_w_down': nrm(ks[3], (L, D_FF, D_MODEL), BETA * D_FF ** -0.5),
        'ln1_w': 1.0 + nrm(ks[4], (L, D_MODEL), 0.02),
        'ln1_b': nrm(ks[5], (L, D_MODEL), 0.02),
        'w_in': nrm(ks[6], (L, D_MODEL, D_IN), D_MODEL ** -0.5),
        'if_bias': jnp.concatenate([i_bias, f_bias], axis=-1),
        'conv_w': nrm(ks[7], (L, CONV_W, 2 * WM_QK), CONV_W ** -0.5),
        'conv_b': nrm(ks[10], (L, 2 * WM_QK), 0.02),
        'norm_m_w': 1.0 + nrm(ks[11], (L, WM_V), 0.02),
        'norm_r_w': 1.0 + nrm(ks[12], (L, WR_V), 0.02),
        'w_proj_m': nrm(ks[13], (L, WM_V, D_MODEL), BETA * WM_V ** -0.5),
        'w_proj_r': nrm(ks[14], (L, WR_V, D_MODEL), BETA * WR_V ** -0.5),
        'w_out': nrm(ks[15], (L, D_MODEL, D_MODEL), BETA * D_MODEL ** -0.5),
        'ln2_w': 1.0 + nrm(ks[16], (L, D_MODEL), 0.02),
        'ln2_b': nrm(ks[17], (L, D_MODEL), 0.02),
        'ffn2_w_gate': nrm(ks[18], (L, D_MODEL, D_FF), D_MODEL ** -0.5),
        'ffn2_w_up': nrm(ks[19], (L, D_MODEL, D_FF), D_MODEL ** -0.5),
        'ffn2_w_down': nrm(ks[20], (L, D_FF, D_MODEL), BETA * D_FF ** -0.5),
        'ln3_w': 1.0 + nrm(ks[21], (L, D_MODEL), 0.02),
        'ln3_b': nrm(ks[22], (L, D_MODEL), 0.02),
    }


def reference(x, positions, ffn1_w_gate, ffn1_w_up, ffn1_w_down, ln1_w, ln1_b, w_in, if_bias, conv_w, conv_b,
              norm_m_w, norm_r_w, w_proj_m, w_proj_r, w_out, ln2_w, ln2_b, ffn2_w_gate, ffn2_w_up, ffn2_w_down,
              ln3_w, ln3_b):
    h = x
    for l in range(DEPTH):
        h = layer_norm(ALPHA * h + 0.5 * swiglu(h, ffn1_w_gate[l], ffn1_w_up[l], ffn1_w_down[l]), ln1_w[l], ln1_b[l])
        mix = token_mixer(h, positions, w_in[l], if_bias[l], conv_w[l], conv_b[l], norm_m_w[l], norm_r_w[l],
                          w_proj_m[l], w_proj_r[l], w_out[l])
        h = layer_norm(ALPHA * h + mix, ln2_w[l], ln2_b[l])
        h = layer_norm(ALPHA * h + 0.5 * swiglu(h, ffn2_w_gate[l], ffn2_w_up[l], ffn2_w_down[l]), ln3_w[l], ln3_b[l])
    return h
```

```python
import functools
import math

import jax
import jax.numpy as jnp
from jax import lax
from jax.experimental import pallas as pl
from jax.experimental.pallas import tpu as pltpu

D_MODEL = 4096
SEQ = 8192
NH = 4
DQK = 256
DV = 512
CONV_W = 4
D_FF = 11008
D_FF_PAD = 11264
CHUNK = 128
LN_EPS = 1e-5
HEAD_EPS = 1e-6
ALPHA = 2.0 ** 0.25
ROPE_BASE = 10000.0

F32 = jnp.float32
BF16 = jnp.bfloat16

VMEM_LIMIT_BYTES = 56 * 1024 * 1024


def _params(sem):
    return pltpu.CompilerParams(dimension_semantics=sem, vmem_limit_bytes=VMEM_LIMIT_BYTES)


def _swiglu_up_body(x_ref, wg_ref, wu_ref, o_ref):
    x = x_ref[...]
    g = jnp.dot(x, wg_ref[...], preferred_element_type=F32)
    u = jnp.dot(x, wu_ref[...], preferred_element_type=F32)
    o_ref[...] = (g * jax.nn.sigmoid(g) * u).astype(o_ref.dtype)


def _swiglu_up(xb, wg, wu, *, bm=1024, bn=512):
    m, k = xb.shape
    n = wg.shape[1]
    return pl.pallas_call(
        _swiglu_up_body,
        grid=(n // bn, m // bm),
        in_specs=[
            pl.BlockSpec((bm, k), lambda j, i: (i, 0)),
            pl.BlockSpec((k, bn), lambda j, i: (0, j)),
            pl.BlockSpec((k, bn), lambda j, i: (0, j)),
        ],
        out_specs=pl.BlockSpec((bm, bn), lambda j, i: (i, j)),
        out_shape=jax.ShapeDtypeStruct((m, n), BF16),
        compiler_params=_params(("parallel", "parallel")),
        name="swiglu_up",
    )(xb, wg, wu)


def _mm_resid_body(x_ref, w_ref, r_ref, o_ref, acc_ref, *, scale, nk):
    kk = pl.program_id(2)

    @pl.when(kk == 0)
    def _():
        acc_ref[...] = jnp.zeros_like(acc_ref)

    acc_ref[...] += jnp.dot(x_ref[...], w_ref[...], preferred_element_type=F32)

    @pl.when(kk == nk - 1)
    def _():
        o_ref[...] = ALPHA * r_ref[...] + scale * acc_ref[...]


def _mm_resid(xb, w, resid, *, scale, bm=1024, bn=1024, bk):
    m, k = xb.shape
    n = w.shape[1]
    nk = k // bk
    return pl.pallas_call(
        functools.partial(_mm_resid_body, scale=scale, nk=nk),
        grid=(m // bm, n // bn, nk),
        in_specs=[
            pl.BlockSpec((bm, bk), lambda i, j, kk: (i, kk)),
            pl.BlockSpec((bk, bn), lambda i, j, kk: (kk, j)),
            pl.BlockSpec((bm, bn), lambda i, j, kk: (i, j)),
        ],
        out_specs=pl.BlockSpec((bm, bn), lambda i, j, kk: (i, j)),
        out_shape=jax.ShapeDtypeStruct((m, n), F32),
        scratch_shapes=[pltpu.VMEM((bm, bn), F32)],
        compiler_params=_params(("parallel", "parallel", "arbitrary")),
        name="mm_resid",
    )(xb, w, resid)


def _mm_body(x_ref, w_ref, o_ref):
    o_ref[...] = jnp.dot(x_ref[...], w_ref[...], preferred_element_type=F32).astype(o_ref.dtype)


def _mm(xb, w, *, out_dtype, bm=1024, bn=1024):
    m, k = xb.shape
    n = w.shape[1]
    bn = min(bn, n)
    return pl.pallas_call(
        _mm_body,
        grid=(n // bn, m // bm),
        in_specs=[
            pl.BlockSpec((bm, k), lambda j, i: (i, 0)),
            pl.BlockSpec((k, bn), lambda j, i: (0, j)),
        ],
        out_specs=pl.BlockSpec((bm, bn), lambda j, i: (i, j)),
        out_shape=jax.ShapeDtypeStruct((m, n), out_dtype),
        compiler_params=_params(("parallel", "parallel")),
        name="mm",
    )(xb, w)


def _merge_body(ym_ref, yr_ref, wm_ref, wr_ref, ga_ref, gb_ref, o_ref):
    pm = jnp.dot(ym_ref[...], wm_ref[...], preferred_element_type=F32)
    pr = jnp.dot(yr_ref[...], wr_ref[...], preferred_element_type=F32)
    ga = jax.nn.sigmoid(ga_ref[...].astype(F32))
    gb = jax.nn.sigmoid(gb_ref[...].astype(F32))
    o_ref[...] = (ga * pm + gb * pr).astype(o_ref.dtype)


def _merge(ym, yr, wm, wr, ycat, *, ga_col0, gb_col0, bm=1024, bn=1024):
    m, k = ym.shape
    n = wm.shape[1]
    ga_blk = ga_col0 // bn
    gb_blk = gb_col0 // bn
    return pl.pallas_call(
        _merge_body,
        grid=(n // bn, m // bm),
        in_specs=[
            pl.BlockSpec((bm, k), lambda j, i: (i, 0)),
            pl.BlockSpec((bm, k), lambda j, i: (i, 0)),
            pl.BlockSpec((k, bn), lambda j, i: (0, j)),
            pl.BlockSpec((k, bn), lambda j, i: (0, j)),
            pl.BlockSpec((bm, bn), lambda j, i: (i, ga_blk + j)),
            pl.BlockSpec((bm, bn), lambda j, i: (i, gb_blk + j)),
        ],
        out_specs=pl.BlockSpec((bm, bn), lambda j, i: (i, j)),
        out_shape=jax.ShapeDtypeStruct((m, n), BF16),
        compiler_params=_params(("parallel", "parallel")),
        name="merge",
    )(ym, yr, wm, wr, ycat, ycat)


def _ln_body(y_ref, w_ref, b_ref, of_ref, ob_ref):
    y = y_ref[...]
    mu = jnp.mean(y, axis=-1, keepdims=True)
    d = y - mu
    var = jnp.mean(d * d, axis=-1, keepdims=True)
    out = d * lax.rsqrt(var + LN_EPS) * w_ref[...] + b_ref[...]
    of_ref[...] = out
    ob_ref[...] = out.astype(BF16)


def _layer_norm(y, w, b, *, bm=256):
    m, d = y.shape
    return pl.pallas_call(
        _ln_body,
        grid=(m // bm,),
        in_specs=[
            pl.BlockSpec((bm, d), lambda i: (i, 0)),
            pl.BlockSpec((1, d), lambda i: (0, 0)),
            pl.BlockSpec((1, d), lambda i: (0, 0)),
        ],
        out_specs=[
            pl.BlockSpec((bm, d), lambda i: (i, 0)),
            pl.BlockSpec((bm, d), lambda i: (i, 0)),
        ],
        out_shape=[jax.ShapeDtypeStruct((m, d), F32), jax.ShapeDtypeStruct((m, d), BF16)],
        compiler_params=_params(("parallel",)),
        name="layer_norm",
    )(y, w.reshape(1, d), b.reshape(1, d))


def _head_norm(hh, w):
    mu = jnp.mean(hh, axis=-1, keepdims=True)
    d = hh - mu
    var = jnp.mean(d * d, axis=-1, keepdims=True)
    return d * lax.rsqrt(var + HEAD_EPS) * w


def _log_sigmoid(x):
    return jnp.minimum(x, 0.0) - jnp.log1p(jnp.exp(-jnp.abs(x)))


def _conv_silu(cur_ref, prev_ref, w_ref, b_ref, first_chunk):
    cur = cur_ref[...].astype(F32)
    prev = jnp.where(first_chunk, 0.0, prev_ref[...].astype(F32))
    row = lax.broadcasted_iota(jnp.int32, (CHUNK, 1), 0)
    w = w_ref[...]
    acc = cur * w[CONV_W - 1:CONV_W, :] + b_ref[...]
    for j in range(1, CONV_W):
        shifted = jnp.where(row < j, pltpu.roll(prev, j, 0), pltpu.roll(cur, j, 0))
        acc = acc + shifted * w[CONV_W - 1 - j:CONV_W - j, :]
    return acc * jax.nn.sigmoid(acc)


def _mlstm_body(bias_ref, q_ref, qp_ref, k_ref, kp_ref, v_ref, o_ref, gir_ref, gfr_ref, gic_ref,
                cwq_ref, cbq_ref, cwk_ref, cbk_ref, nw_ref, y_ref, c_sc, n_sc, m_sc):
    h = pl.program_id(0)
    c = pl.program_id(1)

    @pl.when(c == 0)
    def _():
        c_sc[...] = jnp.zeros_like(c_sc)
        n_sc[...] = jnp.zeros_like(n_sc)
        m_sc[...] = jnp.zeros_like(m_sc)

    first = c == 0
    q = _conv_silu(q_ref, qp_ref, cwq_ref, cbq_ref, first) * (DQK ** -0.5)
    k = _conv_silu(k_ref, kp_ref, cwk_ref, cbk_ref, first)
    v = v_ref[...]

    bias_i = bias_ref[h]
    bias_f = bias_ref[NH + h]
    gi_r = gir_ref[...] + bias_i
    lf_r = _log_sigmoid(gfr_ref[...] + bias_f)
    gi_c = gic_ref[...] + bias_i

    li = lax.broadcasted_iota(jnp.int32, (CHUNK, CHUNK), 0)
    si = lax.broadcasted_iota(jnp.int32, (CHUNK, CHUNK), 1)
    causal = si <= li
    lhs = jnp.where(causal, lf_r, 0.0)
    strict = jnp.where(li > si, 1.0, 0.0).astype(F32)
    seg = jnp.dot(lhs, strict, preferred_element_type=F32, precision=lax.Precision.HIGHEST)
    b_c = jnp.sum(lhs, axis=-1, keepdims=True)

    m_prev = m_sc[:, :1]
    dmat = jnp.where(causal, seg + gi_r, -jnp.inf)
    inter = b_c + m_prev
    m_t = jnp.maximum(inter, jnp.max(dmat, axis=-1, keepdims=True))
    w_intra = jnp.exp(dmat - m_t)
    w_inter = jnp.exp(inter - m_t)

    qb = q.astype(BF16)
    kb = k.astype(BF16)
    s = lax.dot_general(qb, kb, (((1,), (1,)), ((), ())), preferred_element_type=F32) * w_intra
    c_state = c_sc[...]
    n_state = n_sc[...]
    num = jnp.dot(s.astype(BF16), v, preferred_element_type=F32)
    num = num + w_inter * jnp.dot(qb, c_state.astype(BF16), preferred_element_type=F32)
    den = jnp.sum(s, axis=-1, keepdims=True) + w_inter * jnp.sum(q * n_state, axis=-1, keepdims=True)
    hh = num * (1.0 / jnp.maximum(jnp.abs(den), jnp.exp(-m_t)))

    m_new = m_t[CHUNK - 1:CHUNK, :]
    b_last = b_c[CHUNK - 1:CHUNK, :]
    w_k = jnp.exp(b_last - b_c + gi_c - m_new)
    decay = jnp.exp(b_last + m_prev - m_new)
    kw = k * w_k
    c_sc[...] = decay * c_state + jnp.dot(kw.T.astype(BF16), v, preferred_element_type=F32)
    n_sc[...] = decay * n_state + jnp.sum(kw, axis=0, keepdims=True)
    m_sc[...] = jnp.broadcast_to(m_new, m_sc.shape)

    out = _head_norm(hh, nw_ref[...]) * jax.nn.sigmoid(o_ref[...].astype(F32))
    y_ref[...] = out.astype(y_ref.dtype)


def _mlstm(ycat, gi_row, gf_row, gi_col, if_bias, conv_w, conv_b, norm_w):
    s_len = ycat.shape[0]
    nc = s_len // CHUNK
    prev = lambda c: jnp.maximum(c - 1, 0)
    kq = NH
    kv = NH
    ko = 2 * NH
    return pl.pallas_call(
        _mlstm_body,
        grid=(NH, nc),
        in_specs=[
            pl.BlockSpec(memory_space=pltpu.SMEM),
            pl.BlockSpec((CHUNK, DQK), lambda h, c: (c, h)),
            pl.BlockSpec((CHUNK, DQK), lambda h, c: (prev(c), h)),
            pl.BlockSpec((CHUNK, DQK), lambda h, c: (c, kq + h)),
            pl.BlockSpec((CHUNK, DQK), lambda h, c: (prev(c), kq + h)),
            pl.BlockSpec((CHUNK, DV), lambda h, c: (c, kv + h)),
            pl.BlockSpec((CHUNK, DV), lambda h, c: (c, ko + h)),
            pl.BlockSpec((None, 1, CHUNK), lambda h, c: (h, 0, c)),
            pl.BlockSpec((None, 1, CHUNK), lambda h, c: (h, 0, c)),
            pl.BlockSpec((None, CHUNK, 1), lambda h, c: (h, c, 0)),
            pl.BlockSpec((CONV_W, DQK), lambda h, c: (0, h)),
            pl.BlockSpec((1, DQK), lambda h, c: (0, h)),
            pl.BlockSpec((CONV_W, DQK), lambda h, c: (0, kq + h)),
            pl.BlockSpec((1, DQK), lambda h, c: (0, kq + h)),
            pl.BlockSpec((1, DV), lambda h, c: (0, h)),
        ],
        out_specs=pl.BlockSpec((CHUNK, DV), lambda h, c: (c, h)),
        out_shape=jax.ShapeDtypeStruct((s_len, NH * DV), BF16),
        scratch_shapes=[
            pltpu.VMEM((DQK, DV), F32),
            pltpu.VMEM((1, DQK), F32),
            pltpu.VMEM((1, CHUNK), F32),
        ],
        compiler_params=_params(("parallel", "arbitrary")),
        name="mlstm",
    )(if_bias, ycat, ycat, ycat, ycat, ycat, ycat, gi_row, gf_row, gi_col,
      conv_w, conv_b.reshape(1, -1), conv_w, conv_b.reshape(1, -1), norm_w.reshape(1, -1))


def _rope(x, cos, sin):
    half = DQK // 2
    x1 = x[:, :half]
    x2 = x[:, half:]
    return jnp.concatenate([x1 * cos - x2 * sin, x2 * cos + x1 * sin], axis=-1)


def _retention_body(lg_ref, pos_ref, inv_ref, q_ref, k_ref, v_ref, g_ref, nw_ref, y_ref, r_sc):
    h = pl.program_id(0)
    c = pl.program_id(1)

    @pl.when(c == 0)
    def _():
        r_sc[...] = jnp.zeros_like(r_sc)

    ang = pos_ref[...].astype(F32) * inv_ref[...]
    cos = jnp.cos(ang)
    sin = jnp.sin(ang)
    q = _rope(q_ref[...].astype(F32), cos, sin)
    k = _rope(k_ref[...].astype(F32), cos, sin) * (DQK ** -0.5)
    v = v_ref[...]

    lg = lg_ref[h]
    li = lax.broadcasted_iota(jnp.int32, (CHUNK, CHUNK), 0)
    si = lax.broadcasted_iota(jnp.int32, (CHUNK, CHUNK), 1)
    diff = (li - si).astype(F32)
    decay_mask = jnp.where(diff >= 0.0, jnp.exp(lg * jnp.maximum(diff, 0.0)), 0.0)
    pos_c = lax.broadcasted_iota(jnp.int32, (CHUNK, 1), 0).astype(F32)
    q_decay = jnp.exp(lg * (pos_c + 1.0))
    k_decay = jnp.exp(lg * (CHUNK - 1.0 - pos_c))
    chunk_decay = jnp.exp(jnp.full((1, 1), CHUNK, F32) * lg)

    qb = q.astype(BF16)
    s = lax.dot_general(qb, k.astype(BF16), (((1,), (1,)), ((), ())), preferred_element_type=F32) * decay_mask
    r_state = r_sc[...]
    out = jnp.dot(s.astype(BF16), v, preferred_element_type=F32)
    out = out + jnp.dot((q * q_decay).astype(BF16), r_state.astype(BF16), preferred_element_type=F32)
    r_sc[...] = chunk_decay * r_state + jnp.dot((k * k_decay).T.astype(BF16), v, preferred_element_type=F32)

    g = g_ref[...].astype(F32)
    y = _head_norm(out, nw_ref[...]) * (g * jax.nn.sigmoid(g))
    y_ref[...] = y.astype(y_ref.dtype)


def _retention(ycat, positions_col, inv_freq, log_gamma, norm_w, *, col0):
    s_len = ycat.shape[0]
    nc = s_len // CHUNK
    qb0 = col0 // DQK
    kb0 = qb0 + NH
    vb0 = (col0 + 2 * NH * DQK) // DV
    gb0 = vb0 + NH
    return pl.pallas_call(
        _retention_body,
        grid=(NH, nc),
        in_specs=[
            pl.BlockSpec(memory_space=pltpu.SMEM),
            pl.BlockSpec((CHUNK, 1), lambda h, c: (c, 0)),
            pl.BlockSpec((1, DQK // 2), lambda h, c: (0, 0)),
            pl.BlockSpec((CHUNK, DQK), lambda h, c: (c, qb0 + h)),
            pl.BlockSpec((CHUNK, DQK), lambda h, c: (c, kb0 + h)),
            pl.BlockSpec((CHUNK, DV), lambda h, c: (c, vb0 + h)),
            pl.BlockSpec((CHUNK, DV), lambda h, c: (c, gb0 + h)),
            pl.BlockSpec((1, DV), lambda h, c: (0, h)),
        ],
        out_specs=pl.BlockSpec((CHUNK, DV), lambda h, c: (c, h)),
        out_shape=jax.ShapeDtypeStruct((s_len, NH * DV), BF16),
        scratch_shapes=[pltpu.VMEM((DQK, DV), F32)],
        compiler_params=_params(("parallel", "arbitrary")),
        name="retention",
    )(log_gamma, positions_col, inv_freq, ycat, ycat, ycat, ycat, norm_w.reshape(1, -1))


def _ffn(hf, hb, w_gate, w_up, w_down, ln_w, ln_b):
    pad = D_FF_PAD - D_FF
    wg = jnp.pad(w_gate, ((0, 0), (0, pad))).astype(BF16)
    wu = jnp.pad(w_up, ((0, 0), (0, pad))).astype(BF16)
    wd = jnp.pad(w_down, ((0, pad), (0, 0))).astype(BF16)
    up = _swiglu_up(hb, wg, wu)
    y = _mm_resid(up, wd, hf, scale=0.5, bk=D_FF_PAD // 4)
    return _layer_norm(y, ln_w, ln_b)


def kernel(x, positions, ffn1_w_gate, ffn1_w_up, ffn1_w_down, ln1_w, ln1_b, w_in, if_bias, conv_w, conv_b,
           norm_m_w, norm_r_w, w_proj_m, w_proj_r, w_out, ln2_w, ln2_b, ffn2_w_gate, ffn2_w_up, ffn2_w_down,
           ln3_w, ln3_b):
    xf = x[0]
    h1f, h1b = _ffn(xf, xf.astype(BF16), ffn1_w_gate[0], ffn1_w_up[0], ffn1_w_down[0], ln1_w[0], ln1_b[0])

    wi = w_in[0]
    n_a = 2 * NH * DQK + 2 * NH * DV
    w_cat = jnp.concatenate([wi[:, :n_a], wi[:, n_a + 2 * NH:]], axis=1).astype(BF16)
    w_if = jnp.pad(wi[:, n_a:n_a + 2 * NH], ((0, 0), (0, 128 - 2 * NH))).astype(BF16)
    ycat = _mm(h1b, w_cat, out_dtype=BF16)
    yif = _mm(h1b, w_if, out_dtype=F32)[:, :2 * NH]

    g_rows = yif.T
    gi_row = g_rows[:NH].reshape(NH, 1, SEQ)
    gf_row = g_rows[NH:].reshape(NH, 1, SEQ)
    gi_col = g_rows[:NH].reshape(NH, SEQ, 1)
    y_m = _mlstm(ycat, gi_row, gf_row, gi_col, if_bias[0], conv_w[0], conv_b[0], norm_m_w[0])

    inv_freq = (ROPE_BASE ** (-jnp.arange(0, DQK, 2, dtype=F32) / DQK)).reshape(1, DQK // 2)
    log_gamma = jnp.log(1.0 - 2.0 ** (-5.0 - jnp.arange(NH, dtype=F32)))
    y_r = _retention(ycat, positions.reshape(SEQ, 1), inv_freq, log_gamma, norm_r_w[0], col0=n_a)

    gate_col0 = n_a + 2 * NH * DQK + 2 * NH * DV
    merged = _merge(y_m, y_r, w_proj_m[0].astype(BF16), w_proj_r[0].astype(BF16), ycat,
                    ga_col0=gate_col0, gb_col0=gate_col0 + D_MODEL)
    y2 = _mm_resid(merged, w_out[0].astype(BF16), h1f, scale=1.0, bk=D_MODEL // 2)
    h2f, h2b = _layer_norm(y2, ln2_w[0], ln2_b[0])

    out, _ = _ffn(h2f, h2b, ffn2_w_gate[0], ffn2_w_up[0], ffn2_w_down[0], ln3_w[0], ln3_b[0])
    return out.reshape(1, SEQ, D_MODEL)
```

```python
import functools
import math

import jax
import jax.numpy as jnp
from jax import lax
from jax.experimental import pallas as pl
from jax.experimental.pallas import tpu as pltpu

D_MODEL = 4096
SEQ = 8192
NH = 4
DQK = 256
DV = 512
CONV_W = 4
D_FF = 11008
D_FF_PAD = 11264
CHUNK = 128
LN_EPS = 1e-5
HEAD_EPS = 1e-6
ALPHA = 2.0 ** 0.25
ROPE_BASE = 10000.0

F32 = jnp.float32
BF16 = jnp.bfloat16

VMEM_LIMIT_BYTES = 56 * 1024 * 1024


def _params(sem):
    return pltpu.CompilerParams(dimension_semantics=sem, vmem_limit_bytes=VMEM_LIMIT_BYTES)


def _wres_body(*refs, n_x, pairs, shift, n_extra, epilogue, ck, nb, ms, side):
    n_w = len(pairs)
    pos = 0
    x_refs = refs[pos:pos + n_x]; pos += n_x
    wmain = refs[pos:pos + n_w]; pos += n_w
    wext = refs[pos:pos + n_w] if shift else ()
    pos += n_w if shift else 0
    extra = refs[pos:pos + n_extra]; pos += n_extra
    side_in = refs[pos] if side else None
    pos += 1 if side else 0
    out_ref = refs[pos]; pos += 1
    side_out = refs[pos] if side else None
    pos += 1 if side else 0
    w_sc = refs[pos:pos + n_w]

    j = pl.program_id(0)
    i = pl.program_id(1)
    nxt = j % 2
    row0 = pl.multiple_of(i * ck, ck)
    for w in range(n_w):
        chunk = wmain[w][...]
        if shift:
            wide = jnp.concatenate([chunk, wext[w][...]], axis=1)
            chunk = pltpu.roll(wide, wide.shape[1] - shift, 1)[:, :chunk.shape[1]]
        w_sc[w][nxt, pl.ds(row0, ck), :] = chunk.astype(BF16)

    if side:
        rows, valid = side
        blk = jnp.minimum(j * ms + i, nb * ms - 1)
        r = blk * rows + lax.broadcasted_iota(jnp.int32, (rows, 1), 0)
        side_out[...] = jnp.where(r < valid, side_in[...], 0.0).astype(BF16)

    @pl.when(j >= 1)
    def _():
        cur = 1 - nxt
        dots = [jnp.dot(x_refs[xi][...], w_sc[w][cur], preferred_element_type=F32)
                for w, xi in enumerate(pairs)]
        out_ref[...] = epilogue(dots, [e[...] for e in extra], j - 1).astype(out_ref.dtype)


def _wres_matmul(xs, ws, pairs, epilogue, *, n_out, out_dtype, bn, col0=0, shift=0, extras=(),
                 side=None, bm=1024, name):
    m, k = xs[0].shape
    ms = m // bm
    ck = k // ms
    nb = pl.cdiv(n_out, bn)
    cb0 = col0 // bn

    def row_blk(j, i):
        return jnp.where(j > 0, i, 0)

    in_specs = [pl.BlockSpec((bm, k), lambda j, i: (row_blk(j, i), 0)) for _ in xs]
    in_specs += [pl.BlockSpec((ck, bn), lambda j, i: (i, cb0 + jnp.minimum(j, nb - 1))) for _ in ws]
    args = list(xs) + list(ws)
    if shift:
        eb = bn // 128
        in_specs += [pl.BlockSpec((ck, 128), lambda j, i: (i, (cb0 + jnp.minimum(j, nb - 1) + 1) * eb))
                     for _ in ws]
        args += list(ws)
    for arr, c0 in extras:
        in_specs.append(pl.BlockSpec((bm, bn), lambda j, i, b0=c0 // bn: (row_blk(j, i), b0 + jnp.maximum(j - 1, 0))))
        args.append(arr)
    out_specs = [pl.BlockSpec((bm, bn), lambda j, i: (row_blk(j, i), jnp.maximum(j - 1, 0)))]
    out_shape = [jax.ShapeDtypeStruct((m, n_out), out_dtype)]
    body_side = None
    if side is not None:
        s_arr, s_rows, s_valid, s_out_rows = side
        assert s_out_rows == nb * ms * s_rows
        last_in = s_valid // s_rows - 1
        s_cols = s_arr.shape[1]

        def side_blk(j, i):
            return jnp.minimum(j * ms + i, nb * ms - 1)

        in_specs.append(pl.BlockSpec((s_rows, s_cols), lambda j, i: (jnp.minimum(side_blk(j, i), last_in), 0)))
        args.append(s_arr)
        out_specs.append(pl.BlockSpec((s_rows, s_cols), lambda j, i: (side_blk(j, i), 0)))
        out_shape.append(jax.ShapeDtypeStruct((s_out_rows, s_cols), BF16))
        body_side = (s_rows, s_valid)
    body = functools.partial(_wres_body, n_x=len(xs), pairs=tuple(pairs), shift=shift, n_extra=len(extras),
                             epilogue=epilogue, ck=ck, nb=nb, ms=ms, side=body_side)
    res = pl.pallas_call(
        body,
        grid=(nb + 1, ms),
        in_specs=in_specs,
        out_specs=out_specs,
        out_shape=out_shape,
        scratch_shapes=[pltpu.VMEM((2, k, bn), BF16) for _ in ws],
        compiler_params=_params(("arbitrary", "arbitrary")),
        name=name,
    )(*args)
    return res if side is not None else res[0]


def _swiglu_epilogue(n_valid, bn):
    def fn(dots, extras, jb):
        g, u = dots
        col = jb * bn + lax.broadcasted_iota(jnp.int32, (1, bn), 1)
        return jnp.where(col < n_valid, g * jax.nn.sigmoid(g) * u, 0.0)
    return fn


def _plain_epilogue(dots, extras, jb):
    return dots[0]


def _merge_epilogue(dots, extras, jb):
    pm, pr = dots
    ga, gb = extras
    return jax.nn.sigmoid(ga.astype(F32)) * pm + jax.nn.sigmoid(gb.astype(F32)) * pr


def _mm_resid_body(x_ref, w_ref, r_ref, o_ref, acc_ref, *, scale, nk):
    kk = pl.program_id(2)

    @pl.when(kk == 0)
    def _():
        acc_ref[...] = jnp.zeros_like(acc_ref)

    acc_ref[...] += jnp.dot(x_ref[...], w_ref[...], preferred_element_type=F32)

    @pl.when(kk == nk - 1)
    def _():
        o_ref[...] = ALPHA * r_ref[...] + scale * acc_ref[...]


def _mm_resid(xb, w, resid, *, scale, bm=1024, bn=1024, bk):
    m, k = xb.shape
    n = w.shape[1]
    nk = k // bk
    return pl.pallas_call(
        functools.partial(_mm_resid_body, scale=scale, nk=nk),
        grid=(m // bm, n // bn, nk),
        in_specs=[
            pl.BlockSpec((bm, bk), lambda i, j, kk: (i, kk)),
            pl.BlockSpec((bk, bn), lambda i, j, kk: (kk, j)),
            pl.BlockSpec((bm, bn), lambda i, j, kk: (i, j)),
        ],
        out_specs=pl.BlockSpec((bm, bn), lambda i, j, kk: (i, j)),
        out_shape=jax.ShapeDtypeStruct((m, n), F32),
        scratch_shapes=[pltpu.VMEM((bm, bn), F32)],
        compiler_params=_params(("parallel", "parallel", "arbitrary")),
        name="mm_resid",
    )(xb, w, resid)


def _gates_body(x_ref, w_ref, o_ref):
    o_ref[...] = jnp.dot(x_ref[...], w_ref[...].astype(BF16), preferred_element_type=F32)


def _gates_matmul(xb, w, *, col0, bm=1024):
    m, k = xb.shape
    return pl.pallas_call(
        _gates_body,
        grid=(m // bm,),
        in_specs=[
            pl.BlockSpec((bm, k), lambda i: (i, 0)),
            pl.BlockSpec((k, 128), lambda i: (0, col0 // 128)),
        ],
        out_specs=pl.BlockSpec((bm, 128), lambda i: (i, 0)),
        out_shape=jax.ShapeDtypeStruct((m, 128), F32),
        compiler_params=_params(("parallel",)),
        name="gates_mm",
    )(xb, w)


def _ln_body(y_ref, w_ref, b_ref, of_ref, ob_ref):
    y = y_ref[...]
    mu = jnp.mean(y, axis=-1, keepdims=True)
    d = y - mu
    var = jnp.mean(d * d, axis=-1, keepdims=True)
    out = d * lax.rsqrt(var + LN_EPS) * w_ref[...] + b_ref[...]
    of_ref[...] = out
    ob_ref[...] = out.astype(BF16)


def _layer_norm(y, w, b, *, bm=256):
    m, d = y.shape
    return pl.pallas_call(
        _ln_body,
        grid=(m // bm,),
        in_specs=[
            pl.BlockSpec((bm, d), lambda i: (i, 0)),
            pl.BlockSpec((1, d), lambda i: (0, 0)),
            pl.BlockSpec((1, d), lambda i: (0, 0)),
        ],
        out_specs=[
            pl.BlockSpec((bm, d), lambda i: (i, 0)),
            pl.BlockSpec((bm, d), lambda i: (i, 0)),
        ],
        out_shape=[jax.ShapeDtypeStruct((m, d), F32), jax.ShapeDtypeStruct((m, d), BF16)],
        compiler_params=_params(("parallel",)),
        name="layer_norm",
    )(y, w.reshape(1, d), b.reshape(1, d))


def _head_norm(hh, w):
    mu = jnp.mean(hh, axis=-1, keepdims=True)
    d = hh - mu
    var = jnp.mean(d * d, axis=-1, keepdims=True)
    return d * lax.rsqrt(var + HEAD_EPS) * w


def _log_sigmoid(x):
    return jnp.minimum(x, 0.0) - jnp.log1p(jnp.exp(-jnp.abs(x)))


def _conv_silu(cur_ref, prev_ref, w_ref, b_ref, first_chunk):
    cur = cur_ref[...].astype(F32)
    prev = jnp.where(first_chunk, 0.0, prev_ref[...].astype(F32))
    row = lax.broadcasted_iota(jnp.int32, (CHUNK, 1), 0)
    w = w_ref[...]
    acc = cur * w[CONV_W - 1:CONV_W, :] + b_ref[...]
    for j in range(1, CONV_W):
        shifted = jnp.where(row < j, pltpu.roll(prev, j, 0), pltpu.roll(cur, j, 0))
        acc = acc + shifted * w[CONV_W - 1 - j:CONV_W - j, :]
    return acc * jax.nn.sigmoid(acc)


def _mlstm_body(bias_ref, q_ref, qp_ref, k_ref, kp_ref, v_ref, o_ref, gir_ref, gfr_ref, gic_ref,
                cwq_ref, cbq_ref, cwk_ref, cbk_ref, nw_ref, y_ref, c_sc, n_sc, m_sc):
    h = pl.program_id(0)
    c = pl.program_id(1)

    @pl.when(c == 0)
    def _():
        c_sc[...] = jnp.zeros_like(c_sc)
        n_sc[...] = jnp.zeros_like(n_sc)
        m_sc[...] = jnp.zeros_like(m_sc)

    first = c == 0
    q = _conv_silu(q_ref, qp_ref, cwq_ref, cbq_ref, first) * (DQK ** -0.5)
    k = _conv_silu(k_ref, kp_ref, cwk_ref, cbk_ref, first)
    v = v_ref[...]

    bias_i = bias_ref[h]
    bias_f = bias_ref[NH + h]
    gi_r = gir_ref[...] + bias_i
    lf_r = _log_sigmoid(gfr_ref[...] + bias_f)
    gi_c = gic_ref[...] + bias_i

    li = lax.broadcasted_iota(jnp.int32, (CHUNK, CHUNK), 0)
    si = lax.broadcasted_iota(jnp.int32, (CHUNK, CHUNK), 1)
    causal = si <= li
    lhs = jnp.where(causal, lf_r, 0.0)
    strict = jnp.where(li > si, 1.0, 0.0).astype(F32)
    seg = jnp.dot(lhs, strict, preferred_element_type=F32, precision=lax.Precision.HIGHEST)
    b_c = jnp.sum(lhs, axis=-1, keepdims=True)

    m_prev = m_sc[:, :1]
    dmat = jnp.where(causal, seg + gi_r, -jnp.inf)
    inter = b_c + m_prev
    m_t = jnp.maximum(inter, jnp.max(dmat, axis=-1, keepdims=True))
    w_intra = jnp.exp(dmat - m_t)
    w_inter = jnp.exp(inter - m_t)

    qb = q.astype(BF16)
    kb = k.astype(BF16)
    s = lax.dot_general(qb, kb, (((1,), (1,)), ((), ())), preferred_element_type=F32) * w_intra
    c_state = c_sc[...]
    n_state = n_sc[...]
    num = jnp.dot(s.astype(BF16), v, preferred_element_type=F32)
    num = num + w_inter * jnp.dot(qb, c_state.astype(BF16), preferred_element_type=F32)
    den = jnp.sum(s, axis=-1, keepdims=True) + w_inter * jnp.sum(q * n_state, axis=-1, keepdims=True)
    hh = num * (1.0 / jnp.maximum(jnp.abs(den), jnp.exp(-m_t)))

    m_new = m_t[CHUNK - 1:CHUNK, :]
    b_last = b_c[CHUNK - 1:CHUNK, :]
    w_k = jnp.exp(b_last - b_c + gi_c - m_new)
    decay = jnp.exp(b_last + m_prev - m_new)
    kw = k * w_k
    c_sc[...] = decay * c_state + jnp.dot(kw.T.astype(BF16), v, preferred_element_type=F32)
    n_sc[...] = decay * n_state + jnp.sum(kw, axis=0, keepdims=True)
    m_sc[...] = jnp.broadcast_to(m_new, m_sc.shape)

    out = _head_norm(hh, nw_ref[...]) * jax.nn.sigmoid(o_ref[...].astype(F32))
    y_ref[...] = out.astype(y_ref.dtype)


def _mlstm(ycat, gi_row, gf_row, gi_col, if_bias, conv_w, conv_b, norm_w):
    s_len = ycat.shape[0]
    nc = s_len // CHUNK
    prev = lambda c: jnp.maximum(c - 1, 0)
    kq = NH
    kv = NH
    ko = 2 * NH
    return pl.pallas_call(
        _mlstm_body,
        grid=(NH, nc),
        in_specs=[
            pl.BlockSpec(memory_space=pltpu.SMEM),
            pl.BlockSpec((CHUNK, DQK), lambda h, c: (c, h)),
            pl.BlockSpec((CHUNK, DQK), lambda h, c: (prev(c), h)),
            pl.BlockSpec((CHUNK, DQK), lambda h, c: (c, kq + h)),
            pl.BlockSpec((CHUNK, DQK), lambda h, c: (prev(c), kq + h)),
            pl.BlockSpec((CHUNK, DV), lambda h, c: (c, kv + h)),
            pl.BlockSpec((CHUNK, DV), lambda h, c: (c, ko + h)),
            pl.BlockSpec((None, 1, CHUNK), lambda h, c: (h, 0, c)),
            pl.BlockSpec((None, 1, CHUNK), lambda h, c: (h, 0, c)),
            pl.BlockSpec((None, CHUNK, 1), lambda h, c: (h, c, 0)),
            pl.BlockSpec((CONV_W, DQK), lambda h, c: (0, h)),
            pl.BlockSpec((1, DQK), lambda h, c: (0, h)),
            pl.BlockSpec((CONV_W, DQK), lambda h, c: (0, kq + h)),
            pl.BlockSpec((1, DQK), lambda h, c: (0, kq + h)),
            pl.BlockSpec((1, DV), lambda h, c: (0, h)),
        ],
        out_specs=pl.BlockSpec((CHUNK, DV), lambda h, c: (c, h)),
        out_shape=jax.ShapeDtypeStruct((s_len, NH * DV), BF16),
        scratch_shapes=[
            pltpu.VMEM((DQK, DV), F32),
            pltpu.VMEM((1, DQK), F32),
            pltpu.VMEM((1, CHUNK), F32),
        ],
        compiler_params=_params(("parallel", "arbitrary")),
        name="mlstm",
    )(if_bias, ycat, ycat, ycat, ycat, ycat, ycat, gi_row, gf_row, gi_col,
      conv_w, conv_b.reshape(1, -1), conv_w, conv_b.reshape(1, -1), norm_w.reshape(1, -1))


def _rope(x, cos, sin):
    half = DQK // 2
    x1 = x[:, :half]
    x2 = x[:, half:]
    return jnp.concatenate([x1 * cos - x2 * sin, x2 * cos + x1 * sin], axis=-1)


def _retention_body(lg_ref, pos_ref, inv_ref, q_ref, k_ref, v_ref, g_ref, nw_ref, y_ref, r_sc):
    h = pl.program_id(0)
    c = pl.program_id(1)

    @pl.when(c == 0)
    def _():
        r_sc[...] = jnp.zeros_like(r_sc)

    ang = pos_ref[...].astype(F32) * inv_ref[...]
    cos = jnp.cos(ang)
    sin = jnp.sin(ang)
    q = _rope(q_ref[...].astype(F32), cos, sin)
    k = _rope(k_ref[...].astype(F32), cos, sin) * (DQK ** -0.5)
    v = v_ref[...]

    lg = lg_ref[h]
    li = lax.broadcasted_iota(jnp.int32, (CHUNK, CHUNK), 0)
    si = lax.broadcasted_iota(jnp.int32, (CHUNK, CHUNK), 1)
    diff = (li - si).astype(F32)
    decay_mask = jnp.where(diff >= 0.0, jnp.exp(lg * jnp.maximum(diff, 0.0)), 0.0)
    pos_c = lax.broadcasted_iota(jnp.int32, (CHUNK, 1), 0).astype(F32)
    q_decay = jnp.exp(lg * (pos_c + 1.0))
    k_decay = jnp.exp(lg * (CHUNK - 1.0 - pos_c))
    chunk_decay = jnp.exp(jnp.full((1, 1), CHUNK, F32) * lg)

    qb = q.astype(BF16)
    s = lax.dot_general(qb, k.astype(BF16), (((1,), (1,)), ((), ())), preferred_element_type=F32) * decay_mask
    r_state = r_sc[...]
    out = jnp.dot(s.astype(BF16), v, preferred_element_type=F32)
    out = out + jnp.dot((q * q_decay).astype(BF16), r_state.astype(BF16), preferred_element_type=F32)
    r_sc[...] = chunk_decay * r_state + jnp.dot((k * k_decay).T.astype(BF16), v, preferred_element_type=F32)

    g = g_ref[...].astype(F32)
    y = _head_norm(out, nw_ref[...]) * (g * jax.nn.sigmoid(g))
    y_ref[...] = y.astype(y_ref.dtype)


def _retention(ycat, positions_col, inv_freq, log_gamma, norm_w, *, col0):
    s_len = ycat.shape[0]
    nc = s_len // CHUNK
    qb0 = col0 // DQK
    kb0 = qb0 + NH
    vb0 = (col0 + 2 * NH * DQK) // DV
    gb0 = vb0 + NH
    return pl.pallas_call(
        _retention_body,
        grid=(NH, nc),
        in_specs=[
            pl.BlockSpec(memory_space=pltpu.SMEM),
            pl.BlockSpec((CHUNK, 1), lambda h, c: (c, 0)),
            pl.BlockSpec((1, DQK // 2), lambda h, c: (0, 0)),
            pl.BlockSpec((CHUNK, DQK), lambda h, c: (c, qb0 + h)),
            pl.BlockSpec((CHUNK, DQK), lambda h, c: (c, kb0 + h)),
            pl.BlockSpec((CHUNK, DV), lambda h, c: (c, vb0 + h)),
            pl.BlockSpec((CHUNK, DV), lambda h, c: (c, gb0 + h)),
            pl.BlockSpec((1, DV), lambda h, c: (0, h)),
        ],
        out_specs=pl.BlockSpec((CHUNK, DV), lambda h, c: (c, h)),
        out_shape=jax.ShapeDtypeStruct((s_len, NH * DV), BF16),
        scratch_shapes=[pltpu.VMEM((DQK, DV), F32)],
        compiler_params=_params(("parallel", "arbitrary")),
        name="retention",
    )(log_gamma, positions_col, inv_freq, ycat, ycat, ycat, ycat, norm_w.reshape(1, -1))


def _ffn(hf, hb, w_gate, w_up, w_down, ln_w, ln_b):
    bn = 512
    steps = (D_FF_PAD // bn) * (SEQ // 1024)
    up, wd = _wres_matmul([hb], [w_gate, w_up], (0, 0), _swiglu_epilogue(D_FF, bn), n_out=D_FF_PAD,
                          out_dtype=BF16, bn=bn, side=(w_down, D_FF_PAD // steps, D_FF, D_FF_PAD),
                          name="swiglu_up")
    y = _mm_resid(up, wd, hf, scale=0.5, bk=D_FF_PAD // 4)
    return _layer_norm(y, ln_w, ln_b)


def kernel(x, positions, ffn1_w_gate, ffn1_w_up, ffn1_w_down, ln1_w, ln1_b, w_in, if_bias, conv_w, conv_b,
           norm_m_w, norm_r_w, w_proj_m, w_proj_r, w_out, ln2_w, ln2_b, ffn2_w_gate, ffn2_w_up, ffn2_w_down,
           ln3_w, ln3_b):
    xf = x[0]
    h1f, h1b = _ffn(xf, xf.astype(BF16), ffn1_w_gate[0], ffn1_w_up[0], ffn1_w_down[0], ln1_w[0], ln1_b[0])

    wi = w_in[0]
    n_a = 2 * NH * DQK + 2 * NH * DV
    n_b = 2 * NH * DQK + 2 * NH * DV + 2 * D_MODEL
    ya = _wres_matmul([h1b], [wi], (0,), _plain_epilogue, n_out=n_a, out_dtype=BF16, bn=1024, name="in_proj_a")
    yb = _wres_matmul([h1b], [wi], (0,), _plain_epilogue, n_out=n_b, out_dtype=BF16, bn=1024,
                      col0=n_a, shift=2 * NH, name="in_proj_b")
    yif = _gates_matmul(h1b, wi, col0=n_a)[:, :2 * NH]

    g_rows = yif.T
    gi_row = g_rows[:NH].reshape(NH, 1, SEQ)
    gf_row = g_rows[NH:].reshape(NH, 1, SEQ)
    gi_col = g_rows[:NH].reshape(NH, SEQ, 1)
    y_m = _mlstm(ya, gi_row, gf_row, gi_col, if_bias[0], conv_w[0], conv_b[0], norm_m_w[0])

    inv_freq = (ROPE_BASE ** (-jnp.arange(0, DQK, 2, dtype=F32) / DQK)).reshape(1, DQK // 2)
    log_gamma = jnp.log(1.0 - 2.0 ** (-5.0 - jnp.arange(NH, dtype=F32)))
    y_r = _retention(yb, positions.reshape(SEQ, 1), inv_freq, log_gamma, norm_r_w[0], col0=0)

    gate_col0 = 2 * NH * DQK + 2 * NH * DV
    bn = 512
    steps = (D_MODEL // bn) * (SEQ // 1024)
    merged, wo = _wres_matmul([y_m, y_r], [w_proj_m[0], w_proj_r[0]], (0, 1), _merge_epilogue, n_out=D_MODEL,
                              out_dtype=BF16, bn=bn, extras=((yb, gate_col0), (yb, gate_col0 + D_MODEL)),
                              side=(w_out[0], D_MODEL // steps, D_MODEL, D_MODEL), name="merge")
    y2 = _mm_resid(merged, wo, h1f, scale=1.0, bk=D_MODEL // 2)
    h2f, h2b = _layer_norm(y2, ln2_w[0], ln2_b[0])

    out, _ = _ffn(h2f, h2b, ffn2_w_gate[0], ffn2_w_up[0], ffn2_w_down[0], ln3_w[0], ln3_b[0])
    return out.reshape(1, SEQ, D_MODEL)
```

```python
import functools
import math

import jax
import jax.numpy as jnp
from jax import lax
from jax.experimental import pallas as pl
from jax.experimental.pallas import tpu as pltpu

D_MODEL = 4096
SEQ = 8192
NH = 4
DQK = 256
DV = 512
CONV_W = 4
D_FF = 11008
D_FF_PAD = 11264
CHUNK = 128
LN_EPS = 1e-5
HEAD_EPS = 1e-6
ALPHA = 2.0 ** 0.25
ROPE_BASE = 10000.0

F32 = jnp.float32
BF16 = jnp.bfloat16

VMEM_LIMIT_BYTES = 56 * 1024 * 1024


def _params(sem):
    return pltpu.CompilerParams(dimension_semantics=sem, vmem_limit_bytes=VMEM_LIMIT_BYTES)


def _wres_body(*refs, n_x, pairs, shift, n_extra, epilogue, ck, nb, ms, side, rsplit):
    n_w = len(pairs)
    pos = 0
    x_refs = refs[pos:pos + n_x]; pos += n_x
    wmain = refs[pos:pos + n_w]; pos += n_w
    wext = refs[pos:pos + n_w] if shift else ()
    pos += n_w if shift else 0
    extra = refs[pos:pos + n_extra]; pos += n_extra
    side_in = refs[pos] if side else None
    pos += 1 if side else 0
    out_ref = refs[pos]; pos += 1
    side_out = refs[pos] if side else None
    pos += 1 if side else 0
    w_sc = [refs[pos + 2 * w:pos + 2 * w + 2] for w in range(n_w)]

    j = pl.program_id(0)
    i = pl.program_id(1)
    row0 = pl.multiple_of(i * ck, ck)

    def cast_piece(cast_slot, p, n_pieces):
        cp = ck // n_pieces
        for w in range(n_w):
            chunk = wmain[w][p * cp:(p + 1) * cp, :]
            if shift:
                wide = jnp.concatenate([chunk, wext[w][p * cp:(p + 1) * cp, :]], axis=1)
                chunk = pltpu.roll(wide, wide.shape[1] - shift, 1)[:, :chunk.shape[1]]
            w_sc[w][cast_slot][pl.ds(pl.multiple_of(row0 + p * cp, cp), cp), :] = chunk.astype(BF16)
        if side:
            s_rows, s_valid = side
            sp = s_rows // n_pieces
            blk = jnp.minimum(j * ms + i, nb * ms - 1)
            ridx = blk * s_rows + p * sp + lax.broadcasted_iota(jnp.int32, (sp, 1), 0)
            piece = side_in[p * sp:(p + 1) * sp, :]
            side_out[p * sp:(p + 1) * sp, :] = jnp.where(ridx < s_valid, piece, 0.0).astype(BF16)

    def step(cast_slot, read_slot):
        if read_slot is None:
            cast_piece(cast_slot, 0, 1)
            return
        rb = out_ref.shape[0] // rsplit
        n_pieces = max(1, rsplit // 2)
        for r in range(rsplit):
            rows = slice(r * rb, (r + 1) * rb)
            dots = [jnp.dot(x_refs[xi][rows, :], w_sc[w][read_slot][...], preferred_element_type=F32)
                    for w, xi in enumerate(pairs)]
            out_ref[rows, :] = epilogue(dots, [e[rows, :] for e in extra], j - 1).astype(out_ref.dtype)
            if r < n_pieces:
                cast_piece(cast_slot, r, n_pieces)

    even = j % 2 == 0
    pl.when(j == 0)(lambda: step(0, None))
    pl.when(jnp.logical_and(j >= 1, even))(lambda: step(0, 1))
    pl.when(jnp.logical_not(even))(lambda: step(1, 0))


def _wres_matmul(xs, ws, pairs, epilogue, *, n_out, out_dtype, bn, col0=0, shift=0, extras=(),
                 side=None, bm=1024, rsplit=8, name):
    m, k = xs[0].shape
    ms = m // bm
    ck = k // ms
    nb = pl.cdiv(n_out, bn)
    cb0 = col0 // bn

    def row_blk(j, i):
        return jnp.where(j > 0, i, 0)

    in_specs = [pl.BlockSpec((bm, k), lambda j, i: (row_blk(j, i), 0)) for _ in xs]
    in_specs += [pl.BlockSpec((None, ck, bn), lambda j, i: (0, i, cb0 + jnp.minimum(j, nb - 1))) for _ in ws]
    args = list(xs) + list(ws)
    if shift:
        eb = bn // 128
        in_specs += [pl.BlockSpec((None, ck, 128), lambda j, i: (0, i, (cb0 + jnp.minimum(j, nb - 1) + 1) * eb))
                     for _ in ws]
        args += list(ws)
    for arr, c0 in extras:
        in_specs.append(pl.BlockSpec((bm, bn), lambda j, i, b0=c0 // bn: (row_blk(j, i), b0 + jnp.maximum(j - 1, 0))))
        args.append(arr)
    out_specs = [pl.BlockSpec((bm, bn), lambda j, i: (row_blk(j, i), jnp.maximum(j - 1, 0)))]
    out_shape = [jax.ShapeDtypeStruct((m, n_out), out_dtype)]
    body_side = None
    if side is not None:
        s_arr, s_rows, s_valid, s_out_rows = side
        assert s_out_rows == nb * ms * s_rows
        last_in = s_valid // s_rows - 1
        s_cols = s_arr.shape[2]

        def side_blk(j, i):
            return jnp.minimum(j * ms + i, nb * ms - 1)

        in_specs.append(pl.BlockSpec((None, s_rows, s_cols),
                                     lambda j, i: (0, jnp.minimum(side_blk(j, i), last_in), 0)))
        args.append(s_arr)
        out_specs.append(pl.BlockSpec((s_rows, s_cols), lambda j, i: (side_blk(j, i), 0)))
        out_shape.append(jax.ShapeDtypeStruct((s_out_rows, s_cols), BF16))
        body_side = (s_rows, s_valid)
    body = functools.partial(_wres_body, n_x=len(xs), pairs=tuple(pairs), shift=shift, n_extra=len(extras),
                             epilogue=epilogue, ck=ck, nb=nb, ms=ms, side=body_side, rsplit=rsplit)
    res = pl.pallas_call(
        body,
        grid=(nb + 1, ms),
        in_specs=in_specs,
        out_specs=out_specs,
        out_shape=out_shape,
        scratch_shapes=[pltpu.VMEM((k, bn), BF16) for _ in range(2 * len(ws))],
        compiler_params=_params(("arbitrary", "arbitrary")),
        name=name,
    )(*args)
    return res if side is not None else res[0]


def _swiglu_epilogue(n_valid, bn):
    def fn(dots, extras, jb):
        g, u = dots
        col = jb * bn + lax.broadcasted_iota(jnp.int32, (1, bn), 1)
        return jnp.where(col < n_valid, g * jax.nn.sigmoid(g) * u, 0.0)
    return fn


def _plain_epilogue(dots, extras, jb):
    return dots[0]


def _merge_epilogue(dots, extras, jb):
    pm, pr = dots
    ga, gb = extras
    return jax.nn.sigmoid(ga.astype(F32)) * pm + jax.nn.sigmoid(gb.astype(F32)) * pr


def _mm_resid_body(x_ref, w_ref, r_ref, o_ref, acc_ref, *, scale, nk):
    kk = pl.program_id(2)

    @pl.when(kk == 0)
    def _():
        acc_ref[...] = jnp.zeros_like(acc_ref)

    acc_ref[...] += jnp.dot(x_ref[...], w_ref[...], preferred_element_type=F32)

    @pl.when(kk == nk - 1)
    def _():
        o_ref[...] = ALPHA * r_ref[...] + scale * acc_ref[...]


def _mm_resid(xb, w, resid, *, scale, bm=1024, bn=1024, bk):
    m, k = xb.shape
    n = w.shape[1]
    nk = k // bk
    return pl.pallas_call(
        functools.partial(_mm_resid_body, scale=scale, nk=nk),
        grid=(m // bm, n // bn, nk),
        in_specs=[
            pl.BlockSpec((bm, bk), lambda i, j, kk: (i, kk)),
            pl.BlockSpec((bk, bn), lambda i, j, kk: (kk, j)),
            pl.BlockSpec((bm, bn), lambda i, j, kk: (i, j)),
        ],
        out_specs=pl.BlockSpec((bm, bn), lambda i, j, kk: (i, j)),
        out_shape=jax.ShapeDtypeStruct((m, n), F32),
        scratch_shapes=[pltpu.VMEM((bm, bn), F32)],
        compiler_params=_params(("parallel", "parallel", "arbitrary")),
        name="mm_resid",
    )(xb, w, resid)


def _gates_body(x_ref, w_ref, o_ref):
    o_ref[...] = jnp.dot(x_ref[...], w_ref[...].astype(BF16), preferred_element_type=F32)


def _gates_matmul(xb, w, *, col0, bm=1024):
    m, k = xb.shape
    return pl.pallas_call(
        _gates_body,
        grid=(m // bm,),
        in_specs=[
            pl.BlockSpec((bm, k), lambda i: (i, 0)),
            pl.BlockSpec((None, k, 128), lambda i: (0, 0, col0 // 128)),
        ],
        out_specs=pl.BlockSpec((bm, 128), lambda i: (i, 0)),
        out_shape=jax.ShapeDtypeStruct((m, 128), F32),
        compiler_params=_params(("parallel",)),
        name="gates_mm",
    )(xb, w)


def _ln_body(y_ref, w_ref, b_ref, of_ref, *maybe_ob_ref):
    y = y_ref[...]
    mu = jnp.mean(y, axis=-1, keepdims=True)
    d = y - mu
    var = jnp.mean(d * d, axis=-1, keepdims=True)
    out = d * lax.rsqrt(var + LN_EPS) * w_ref[...] + b_ref[...]
    of_ref[...] = out
    for ob_ref in maybe_ob_ref:
        ob_ref[...] = out.astype(BF16)


def _layer_norm(y, w, b, *, with_bf16, bm=256):
    m, d = y.shape
    dtypes = (F32, BF16) if with_bf16 else (F32,)
    return pl.pallas_call(
        _ln_body,
        grid=(m // bm,),
        in_specs=[
            pl.BlockSpec((bm, d), lambda i: (i, 0)),
            pl.BlockSpec((1, d), lambda i: (0, 0)),
            pl.BlockSpec((1, d), lambda i: (0, 0)),
        ],
        out_specs=[pl.BlockSpec((bm, d), lambda i: (i, 0)) for _ in dtypes],
        out_shape=[jax.ShapeDtypeStruct((m, d), dt) for dt in dtypes],
        compiler_params=_params(("parallel",)),
        name="layer_norm",
    )(y, w.reshape(1, d), b.reshape(1, d))


def _head_norm(hh, w):
    mu = jnp.mean(hh, axis=-1, keepdims=True)
    d = hh - mu
    var = jnp.mean(d * d, axis=-1, keepdims=True)
    return d * lax.rsqrt(var + HEAD_EPS) * w


def _log_sigmoid(x):
    return jnp.minimum(x, 0.0) - jnp.log1p(jnp.exp(-jnp.abs(x)))


def _conv_silu(cur_ref, prev_ref, w, b, first_chunk):
    cur = cur_ref[...].astype(F32)
    prev = jnp.where(first_chunk, 0.0, prev_ref[...].astype(F32))
    row = lax.broadcasted_iota(jnp.int32, (CHUNK, 1), 0)
    acc = cur * w[CONV_W - 1:CONV_W, :] + b
    for j in range(1, CONV_W):
        shifted = jnp.where(row < j, pltpu.roll(prev, j, 0), pltpu.roll(cur, j, 0))
        acc = acc + shifted * w[CONV_W - 1 - j:CONV_W - j, :]
    return acc * jax.nn.sigmoid(acc)


def _mlstm_body(bias_ref, q_ref, qp_ref, k_ref, kp_ref, v_ref, o_ref, gir_ref, gfr_ref, gic_ref,
                cw_ref, cb_ref, nw_ref, y_ref, c_sc, n_sc, m_sc):
    c = pl.program_id(0)

    @pl.when(c == 0)
    def _():
        c_sc[...] = jnp.zeros_like(c_sc)
        n_sc[...] = jnp.zeros_like(n_sc)
        m_sc[...] = jnp.zeros_like(m_sc)

    first = c == 0
    nqk = NH * DQK
    cw = cw_ref[...]
    cb = cb_ref[...]
    q_all = _conv_silu(q_ref, qp_ref, cw[:, :nqk], cb[:, :nqk], first) * (DQK ** -0.5)
    k_all = _conv_silu(k_ref, kp_ref, cw[:, nqk:], cb[:, nqk:], first)

    li = lax.broadcasted_iota(jnp.int32, (CHUNK, CHUNK), 0)
    si = lax.broadcasted_iota(jnp.int32, (CHUNK, CHUNK), 1)
    causal = si <= li
    strict = jnp.where(li > si, 1.0, 0.0).astype(F32)

    for h in range(NH):
        q = q_all[:, h * DQK:(h + 1) * DQK]
        k = k_all[:, h * DQK:(h + 1) * DQK]
        v = v_ref[:, h * DV:(h + 1) * DV]
        bias_i = bias_ref[h]
        bias_f = bias_ref[NH + h]
        gi_r = gir_ref[h] + bias_i
        lf_r = _log_sigmoid(gfr_ref[h] + bias_f)
        gi_c = gic_ref[h] + bias_i

        lhs = jnp.where(causal, lf_r, 0.0)
        seg = jnp.dot(lhs, strict, preferred_element_type=F32, precision=lax.Precision.HIGHEST)
        b_c = jnp.sum(lhs, axis=-1, keepdims=True)

        m_prev = m_sc[h][:, :1]
        dmat = jnp.where(causal, seg + gi_r, -jnp.inf)
        inter = b_c + m_prev
        m_t = jnp.maximum(inter, jnp.max(dmat, axis=-1, keepdims=True))
        w_intra = jnp.exp(dmat - m_t)
        w_inter = jnp.exp(inter - m_t)

        qb = q.astype(BF16)
        kb = k.astype(BF16)
        s = lax.dot_general(qb, kb, (((1,), (1,)), ((), ())), preferred_element_type=F32) * w_intra
        c_state = c_sc[h]
        n_state = n_sc[h]
        num = jnp.dot(s.astype(BF16), v, preferred_element_type=F32)
        num = num + w_inter * jnp.dot(qb, c_state.astype(BF16), preferred_element_type=F32)
        den = jnp.sum(s, axis=-1, keepdims=True) + w_inter * jnp.sum(q * n_state, axis=-1, keepdims=True)
        hh = num * (1.0 / jnp.maximum(jnp.abs(den), jnp.exp(-m_t)))

        m_new = m_t[CHUNK - 1:CHUNK, :]
        b_last = b_c[CHUNK - 1:CHUNK, :]
        w_k = jnp.exp(b_last - b_c + gi_c - m_new)
        decay = jnp.exp(b_last + m_prev - m_new)
        kw = k * w_k
        c_sc[h] = decay * c_state + jnp.dot(kw.T.astype(BF16), v, preferred_element_type=F32)
        n_sc[h] = decay * n_state + jnp.sum(kw, axis=0, keepdims=True)
        m_sc[h] = jnp.broadcast_to(m_new, (1, CHUNK))

        gate = jax.nn.sigmoid(o_ref[:, h * DV:(h + 1) * DV].astype(F32))
        out = _head_norm(hh, nw_ref[:, h * DV:(h + 1) * DV]) * gate
        y_ref[:, h * DV:(h + 1) * DV] = out.astype(y_ref.dtype)


def _mlstm(ya, gi_row, gf_row, gi_col, if_bias, conv_w, conv_b, norm_w):
    s_len = ya.shape[0]
    nc = s_len // CHUNK
    nqk = NH * DQK
    nv = NH * DV
    prev = lambda c: jnp.maximum(c - 1, 0)
    return pl.pallas_call(
        _mlstm_body,
        grid=(nc,),
        in_specs=[
            pl.BlockSpec(memory_space=pltpu.SMEM),
            pl.BlockSpec((CHUNK, nqk), lambda c: (c, 0)),
            pl.BlockSpec((CHUNK, nqk), lambda c: (prev(c), 0)),
            pl.BlockSpec((CHUNK, nqk), lambda c: (c, 1)),
            pl.BlockSpec((CHUNK, nqk), lambda c: (prev(c), 1)),
            pl.BlockSpec((CHUNK, nv), lambda c: (c, 1)),
            pl.BlockSpec((CHUNK, nv), lambda c: (c, 2)),
            pl.BlockSpec((NH, 1, CHUNK), lambda c: (0, 0, c)),
            pl.BlockSpec((NH, 1, CHUNK), lambda c: (0, 0, c)),
            pl.BlockSpec((NH, CHUNK, 1), lambda c: (0, c, 0)),
            pl.BlockSpec((CONV_W, 2 * nqk), lambda c: (0, 0)),
            pl.BlockSpec((1, 2 * nqk), lambda c: (0, 0)),
            pl.BlockSpec((1, nv), lambda c: (0, 0)),
        ],
        out_specs=pl.BlockSpec((CHUNK, nv), lambda c: (c, 0)),
        out_shape=jax.ShapeDtypeStruct((s_len, nv), BF16),
        scratch_shapes=[
            pltpu.VMEM((NH, DQK, DV), F32),
            pltpu.VMEM((NH, 1, DQK), F32),
            pltpu.VMEM((NH, 1, CHUNK), F32),
        ],
        compiler_params=_params(("arbitrary",)),
        name="mlstm",
    )(if_bias, ya, ya, ya, ya, ya, ya, gi_row, gf_row, gi_col, conv_w, conv_b.reshape(1, -1),
      norm_w.reshape(1, -1))


def _rope(x, cos, sin):
    half = DQK // 2
    x1 = x[:, :half]
    x2 = x[:, half:]
    return jnp.concatenate([x1 * cos - x2 * sin, x2 * cos + x1 * sin], axis=-1)


def _retention_body(lg_ref, pos_ref, inv_ref, q_ref, k_ref, v_ref, g_ref, nw_ref, y_ref, r_sc):
    c = pl.program_id(0)

    @pl.when(c == 0)
    def _():
        r_sc[...] = jnp.zeros_like(r_sc)

    ang = pos_ref[...].astype(F32) * inv_ref[...]
    cos = jnp.cos(ang)
    sin = jnp.sin(ang)
    li = lax.broadcasted_iota(jnp.int32, (CHUNK, CHUNK), 0)
    si = lax.broadcasted_iota(jnp.int32, (CHUNK, CHUNK), 1)
    diff = (li - si).astype(F32)
    pos_c = lax.broadcasted_iota(jnp.int32, (CHUNK, 1), 0).astype(F32)

    for h in range(NH):
        q = _rope(q_ref[:, h * DQK:(h + 1) * DQK].astype(F32), cos, sin)
        k = _rope(k_ref[:, h * DQK:(h + 1) * DQK].astype(F32), cos, sin) * (DQK ** -0.5)
        v = v_ref[:, h * DV:(h + 1) * DV]

        lg = lg_ref[h]
        decay_mask = jnp.where(diff >= 0.0, jnp.exp(lg * jnp.maximum(diff, 0.0)), 0.0)
        q_decay = jnp.exp(lg * (pos_c + 1.0))
        k_decay = jnp.exp(lg * (CHUNK - 1.0 - pos_c))
        chunk_decay = jnp.exp(jnp.full((1, 1), CHUNK, F32) * lg)

        qb = q.astype(BF16)
        s = lax.dot_general(qb, k.astype(BF16), (((1,), (1,)), ((), ())), preferred_element_type=F32) * decay_mask
        r_state = r_sc[h]
        out = jnp.dot(s.astype(BF16), v, preferred_element_type=F32)
        out = out + jnp.dot((q * q_decay).astype(BF16), r_state.astype(BF16), preferred_element_type=F32)
        r_sc[h] = chunk_decay * r_state + jnp.dot((k * k_decay).T.astype(BF16), v, preferred_element_type=F32)

        g = g_ref[:, h * DV:(h + 1) * DV].astype(F32)
        y = _head_norm(out, nw_ref[:, h * DV:(h + 1) * DV]) * (g * jax.nn.sigmoid(g))
        y_ref[:, h * DV:(h + 1) * DV] = y.astype(y_ref.dtype)


def _retention(yb, positions_col, inv_freq, log_gamma, norm_w):
    s_len = yb.shape[0]
    nc = s_len // CHUNK
    nqk = NH * DQK
    nv = NH * DV
    return pl.pallas_call(
        _retention_body,
        grid=(nc,),
        in_specs=[
            pl.BlockSpec(memory_space=pltpu.SMEM),
            pl.BlockSpec((CHUNK, 1), lambda c: (c, 0)),
            pl.BlockSpec((1, DQK // 2), lambda c: (0, 0)),
            pl.BlockSpec((CHUNK, nqk), lambda c: (c, 0)),
            pl.BlockSpec((CHUNK, nqk), lambda c: (c, 1)),
            pl.BlockSpec((CHUNK, nv), lambda c: (c, 1)),
            pl.BlockSpec((CHUNK, nv), lambda c: (c, 2)),
            pl.BlockSpec((1, nv), lambda c: (0, 0)),
        ],
        out_specs=pl.BlockSpec((CHUNK, nv), lambda c: (c, 0)),
        out_shape=jax.ShapeDtypeStruct((s_len, nv), BF16),
        scratch_shapes=[pltpu.VMEM((NH, DQK, DV), F32)],
        compiler_params=_params(("arbitrary",)),
        name="retention",
    )(log_gamma, positions_col, inv_freq, yb, yb, yb, yb, norm_w.reshape(1, -1))


def _ffn(hf, hb, w_gate, w_up, w_down, ln_w, ln_b, *, with_bf16):
    bn = 512
    steps = (D_FF_PAD // bn) * (SEQ // 1024)
    up, wd = _wres_matmul([hb], [w_gate, w_up], (0, 0), _swiglu_epilogue(D_FF, bn), n_out=D_FF_PAD,
                          out_dtype=BF16, bn=bn, side=(w_down, D_FF_PAD // steps, D_FF, D_FF_PAD),
                          name="swiglu_up")
    y = _mm_resid(up, wd, hf, scale=0.5, bk=D_FF_PAD // 4)
    return _layer_norm(y, ln_w, ln_b, with_bf16=with_bf16)


def kernel(x, positions, ffn1_w_gate, ffn1_w_up, ffn1_w_down, ln1_w, ln1_b, w_in, if_bias, conv_w, conv_b,
           norm_m_w, norm_r_w, w_proj_m, w_proj_r, w_out, ln2_w, ln2_b, ffn2_w_gate, ffn2_w_up, ffn2_w_down,
           ln3_w, ln3_b):
    xf = x[0]
    h1f, h1b = _ffn(xf, xf.astype(BF16), ffn1_w_gate, ffn1_w_up, ffn1_w_down, ln1_w[0], ln1_b[0], with_bf16=True)

    n_a = 2 * NH * DQK + 2 * NH * DV
    n_b = 2 * NH * DQK + 2 * NH * DV + 2 * D_MODEL
    ya = _wres_matmul([h1b], [w_in], (0,), _plain_epilogue, n_out=n_a, out_dtype=BF16, bn=1024, name="in_proj_a")
    yb = _wres_matmul([h1b], [w_in], (0,), _plain_epilogue, n_out=n_b, out_dtype=BF16, bn=1024,
                      col0=n_a, shift=2 * NH, name="in_proj_b")
    yif = _gates_matmul(h1b, w_in, col0=n_a)[:, :2 * NH]

    g_rows = yif.T
    gi_row = g_rows[:NH].reshape(NH, 1, SEQ)
    gf_row = g_rows[NH:].reshape(NH, 1, SEQ)
    gi_col = g_rows[:NH].reshape(NH, SEQ, 1)
    y_m = _mlstm(ya, gi_row, gf_row, gi_col, if_bias[0], conv_w[0], conv_b[0], norm_m_w[0])

    inv_freq = (ROPE_BASE ** (-jnp.arange(0, DQK, 2, dtype=F32) / DQK)).reshape(1, DQK // 2)
    log_gamma = jnp.log(1.0 - 2.0 ** (-5.0 - jnp.arange(NH, dtype=F32)))
    y_r = _retention(yb, positions.reshape(SEQ, 1), inv_freq, log_gamma, norm_r_w[0])

    gate_col0 = 2 * NH * DQK + 2 * NH * DV
    bn = 512
    steps = (D_MODEL // bn) * (SEQ // 1024)
    merged, wo = _wres_matmul([y_m, y_r], [w_proj_m, w_proj_r], (0, 1), _merge_epilogue, n_out=D_MODEL,
                              out_dtype=BF16, bn=bn, extras=((yb, gate_col0), (yb, gate_col0 + D_MODEL)),
                              side=(w_out, D_MODEL // steps, D_MODEL, D_MODEL), name="merge")
    y2 = _mm_resid(merged, wo, h1f, scale=1.0, bk=D_MODEL // 2)
    h2f, h2b = _layer_norm(y2, ln2_w[0], ln2_b[0], with_bf16=True)

    (out,) = _ffn(h2f, h2b, ffn2_w_gate, ffn2_w_up, ffn2_w_down, ln3_w[0], ln3_b[0], with_bf16=False)
    return out.reshape(1, SEQ, D_MODEL)
```

```python
import functools
import math

import jax
import jax.numpy as jnp
from jax import lax
from jax.experimental import pallas as pl
from jax.experimental.pallas import tpu as pltpu

D_MODEL = 4096
SEQ = 8192
NH = 4
DQK = 256
DV = 512
CONV_W = 4
D_FF = 11008
D_FF_PAD = 11264
CHUNK = 128
LN_EPS = 1e-5
HEAD_EPS = 1e-6
ALPHA = 2.0 ** 0.25
ROPE_BASE = 10000.0

F32 = jnp.float32
BF16 = jnp.bfloat16

VMEM_LIMIT_BYTES = 56 * 1024 * 1024


def _params(sem):
    return pltpu.CompilerParams(dimension_semantics=sem, vmem_limit_bytes=VMEM_LIMIT_BYTES)


def _wres_body(*refs, n_x, pairs, transposed, shift, n_extra, epilogue, ck, nb, ms, side, rsplit):
    n_w = len(pairs)
    pos = 0
    x_refs = refs[pos:pos + n_x]; pos += n_x
    wmain = refs[pos:pos + n_w]; pos += n_w
    wext = refs[pos:pos + n_w] if shift else ()
    pos += n_w if shift else 0
    extra = refs[pos:pos + n_extra]; pos += n_extra
    side_in = refs[pos] if side else None
    pos += 1 if side else 0
    out_ref = refs[pos]; pos += 1
    side_out = refs[pos] if side else None
    pos += 1 if side else 0
    w_sc = [refs[pos + 2 * w:pos + 2 * w + 2] for w in range(n_w)]

    j = pl.program_id(0)
    i = pl.program_id(1)
    row0 = pl.multiple_of(i * ck, ck)

    def cast_piece(cast_slot, p, n_pieces):
        cp = ck // n_pieces
        for w in range(n_w):
            if transposed:
                blk = wmain[w][:, p * cp:(p + 1) * cp]
                if shift:
                    blk = jnp.concatenate([blk[shift:, :], wext[w][:, p * cp:(p + 1) * cp]], axis=0)
                chunk = blk.T
            else:
                chunk = wmain[w][p * cp:(p + 1) * cp, :]
            w_sc[w][cast_slot][pl.ds(pl.multiple_of(row0 + p * cp, cp), cp), :] = chunk.astype(BF16)
        if side:
            s_rows, s_valid = side
            sp = s_rows // n_pieces
            blk = jnp.minimum(j * ms + i, nb * ms - 1)
            ridx = blk * s_rows + p * sp + lax.broadcasted_iota(jnp.int32, (sp, 1), 0)
            piece = side_in[p * sp:(p + 1) * sp, :]
            side_out[p * sp:(p + 1) * sp, :] = jnp.where(ridx < s_valid, piece, 0.0).astype(BF16)

    def step(cast_slot, read_slot):
        if read_slot is None:
            cast_piece(cast_slot, 0, 1)
            return
        rb = out_ref.shape[0] // rsplit
        n_pieces = max(1, rsplit // 2)
        for r in range(rsplit):
            rows = slice(r * rb, (r + 1) * rb)
            dots = [jnp.dot(x_refs[xi][rows, :], w_sc[w][read_slot][...], preferred_element_type=F32)
                    for w, xi in enumerate(pairs)]
            out_ref[rows, :] = epilogue(dots, [e[rows, :] for e in extra], j - 1).astype(out_ref.dtype)
            if r < n_pieces:
                cast_piece(cast_slot, r, n_pieces)

    even = j % 2 == 0
    pl.when(j == 0)(lambda: step(0, None))
    pl.when(jnp.logical_and(j >= 1, even))(lambda: step(0, 1))
    pl.when(jnp.logical_not(even))(lambda: step(1, 0))


def _wres_matmul(xs, ws, pairs, epilogue, *, n_out, out_dtype, bn, transposed=False, col0=0, shift=0,
                 extras=(), side=None, bm=1024, rsplit=8, name):
    m, k = xs[0].shape
    ms = m // bm
    ck = k // ms
    nb = pl.cdiv(n_out, bn)
    cb0 = col0 // bn
    assert not shift or transposed

    def row_blk(j, i):
        return jnp.where(j > 0, i, 0)

    def w_blk(j):
        return cb0 + jnp.minimum(j, nb - 1)

    in_specs = [pl.BlockSpec((bm, k), lambda j, i: (row_blk(j, i), 0)) for _ in xs]
    if transposed:
        in_specs += [pl.BlockSpec((None, bn, ck), lambda j, i: (0, w_blk(j), i)) for _ in ws]
    else:
        in_specs += [pl.BlockSpec((None, ck, bn), lambda j, i: (0, i, w_blk(j))) for _ in ws]
    args = list(xs) + list(ws)
    if shift:
        eb = bn // shift
        in_specs += [pl.BlockSpec((None, shift, ck), lambda j, i: (0, (w_blk(j) + 1) * eb, i)) for _ in ws]
        args += list(ws)
    for arr, c0 in extras:
        in_specs.append(pl.BlockSpec((bm, bn), lambda j, i, b0=c0 // bn: (row_blk(j, i), b0 + jnp.maximum(j - 1, 0))))
        args.append(arr)
    out_specs = [pl.BlockSpec((bm, bn), lambda j, i: (row_blk(j, i), jnp.maximum(j - 1, 0)))]
    out_shape = [jax.ShapeDtypeStruct((m, n_out), out_dtype)]
    body_side = None
    if side is not None:
        s_arr, s_rows, s_valid, s_out_rows = side
        assert s_out_rows == nb * ms * s_rows
        last_in = s_valid // s_rows - 1
        s_cols = s_arr.shape[2]

        def side_blk(j, i):
            return jnp.minimum(j * ms + i, nb * ms - 1)

        in_specs.append(pl.BlockSpec((None, s_rows, s_cols),
                                     lambda j, i: (0, jnp.minimum(side_blk(j, i), last_in), 0)))
        args.append(s_arr)
        out_specs.append(pl.BlockSpec((s_rows, s_cols), lambda j, i: (side_blk(j, i), 0)))
        out_shape.append(jax.ShapeDtypeStruct((s_out_rows, s_cols), BF16))
        body_side = (s_rows, s_valid)
    body = functools.partial(_wres_body, n_x=len(xs), pairs=tuple(pairs), transposed=transposed, shift=shift,
                             n_extra=len(extras),
                             epilogue=epilogue, ck=ck, nb=nb, ms=ms, side=body_side, rsplit=rsplit)
    res = pl.pallas_call(
        body,
        grid=(nb + 1, ms),
        in_specs=in_specs,
        out_specs=out_specs,
        out_shape=out_shape,
        scratch_shapes=[pltpu.VMEM((k, bn), BF16) for _ in range(2 * len(ws))],
        compiler_params=_params(("arbitrary", "arbitrary")),
        name=name,
    )(*args)
    return res if side is not None else res[0]


def _swiglu_epilogue(n_valid, bn):
    def fn(dots, extras, jb):
        g, u = dots
        col = jb * bn + lax.broadcasted_iota(jnp.int32, (1, bn), 1)
        return jnp.where(col < n_valid, g * jax.nn.sigmoid(g) * u, 0.0)
    return fn


def _plain_epilogue(dots, extras, jb):
    return dots[0]


def _merge_epilogue(dots, extras, jb):
    pm, pr = dots
    ga, gb = extras
    return jax.nn.sigmoid(ga.astype(F32)) * pm + jax.nn.sigmoid(gb.astype(F32)) * pr


def _mm_resid_body(x_ref, w_ref, r_ref, o_ref, acc_ref, *, scale, nk, rsplit):
    kk = pl.program_id(2)
    rb = o_ref.shape[0] // rsplit

    def run(first, last):
        for r in range(rsplit):
            rows = slice(r * rb, (r + 1) * rb)
            d = jnp.dot(x_ref[rows, :], w_ref[...], preferred_element_type=F32)
            if not first:
                d = acc_ref[rows, :] + d
            if last:
                o_ref[rows, :] = ALPHA * r_ref[rows, :] + scale * d
            else:
                acc_ref[rows, :] = d

    pl.when(kk == 0)(lambda: run(True, nk == 1))
    if nk > 2:
        pl.when(jnp.logical_and(kk > 0, kk < nk - 1))(lambda: run(False, False))
    if nk > 1:
        pl.when(kk == nk - 1)(lambda: run(False, True))


def _mm_resid(xb, w, resid, *, scale, bm=1024, bn=1024, bk, rsplit=4):
    m, k = xb.shape
    n = w.shape[1]
    nk = k // bk
    return pl.pallas_call(
        functools.partial(_mm_resid_body, scale=scale, nk=nk, rsplit=rsplit),
        grid=(m // bm, n // bn, nk),
        in_specs=[
            pl.BlockSpec((bm, bk), lambda i, j, kk: (i, kk)),
            pl.BlockSpec((bk, bn), lambda i, j, kk: (kk, j)),
            pl.BlockSpec((bm, bn), lambda i, j, kk: (i, j)),
        ],
        out_specs=pl.BlockSpec((bm, bn), lambda i, j, kk: (i, j)),
        out_shape=jax.ShapeDtypeStruct((m, n), F32),
        scratch_shapes=[pltpu.VMEM((bm, bn), F32)],
        compiler_params=_params(("parallel", "parallel", "arbitrary")),
        name="mm_resid",
    )(xb, w, resid)


def _gates_body(x_ref, wt_ref, o_ref):
    o_ref[...] = lax.dot_general(x_ref[...], wt_ref[...].astype(BF16), (((1,), (1,)), ((), ())),
                                 preferred_element_type=F32)


def _gates_matmul(xb, wt, *, col0, n_cols=8, bm=1024):
    m, k = xb.shape
    return pl.pallas_call(
        _gates_body,
        grid=(m // bm,),
        in_specs=[
            pl.BlockSpec((bm, k), lambda i: (i, 0)),
            pl.BlockSpec((None, n_cols, k), lambda i: (0, col0 // n_cols, 0)),
        ],
        out_specs=pl.BlockSpec((bm, n_cols), lambda i: (i, 0)),
        out_shape=jax.ShapeDtypeStruct((m, n_cols), F32),
        compiler_params=_params(("parallel",)),
        name="gates_mm",
    )(xb, wt)


def _ln_body(y_ref, w_ref, b_ref, of_ref, *maybe_ob_ref):
    y = y_ref[...]
    mu = jnp.mean(y, axis=-1, keepdims=True)
    d = y - mu
    var = jnp.mean(d * d, axis=-1, keepdims=True)
    out = d * lax.rsqrt(var + LN_EPS) * w_ref[...] + b_ref[...]
    of_ref[...] = out
    for ob_ref in maybe_ob_ref:
        ob_ref[...] = out.astype(BF16)


def _layer_norm(y, w, b, *, with_bf16, bm=256):
    m, d = y.shape
    dtypes = (F32, BF16) if with_bf16 else (F32,)
    return pl.pallas_call(
        _ln_body,
        grid=(m // bm,),
        in_specs=[
            pl.BlockSpec((bm, d), lambda i: (i, 0)),
            pl.BlockSpec((1, d), lambda i: (0, 0)),
            pl.BlockSpec((1, d), lambda i: (0, 0)),
        ],
        out_specs=[pl.BlockSpec((bm, d), lambda i: (i, 0)) for _ in dtypes],
        out_shape=[jax.ShapeDtypeStruct((m, d), dt) for dt in dtypes],
        compiler_params=_params(("parallel",)),
        name="layer_norm",
    )(y, w.reshape(1, d), b.reshape(1, d))


def _head_norm(hh, w):
    mu = jnp.mean(hh, axis=-1, keepdims=True)
    d = hh - mu
    var = jnp.mean(d * d, axis=-1, keepdims=True)
    return d * lax.rsqrt(var + HEAD_EPS) * w


def _log_sigmoid(x):
    return jnp.minimum(x, 0.0) - jnp.log1p(jnp.exp(-jnp.abs(x)))


def _conv_silu(cur_ref, prev_ref, w, b, first_chunk):
    cur = cur_ref[...].astype(F32)
    prev = jnp.where(first_chunk, 0.0, prev_ref[...].astype(F32))
    row = lax.broadcasted_iota(jnp.int32, (CHUNK, 1), 0)
    acc = cur * w[CONV_W - 1:CONV_W, :] + b
    for j in range(1, CONV_W):
        shifted = jnp.where(row < j, pltpu.roll(prev, j, 0), pltpu.roll(cur, j, 0))
        acc = acc + shifted * w[CONV_W - 1 - j:CONV_W - j, :]
    return acc * jax.nn.sigmoid(acc)


def _mlstm_body(bias_ref, q_ref, qp_ref, k_ref, kp_ref, v_ref, o_ref, gir_ref, gfr_ref, gic_ref,
                cw_ref, cb_ref, nw_ref, y_ref, c_sc, n_sc, m_sc):
    c = pl.program_id(0)

    @pl.when(c == 0)
    def _():
        c_sc[...] = jnp.zeros_like(c_sc)
        n_sc[...] = jnp.zeros_like(n_sc)
        m_sc[...] = jnp.zeros_like(m_sc)

    first = c == 0
    nqk = NH * DQK
    cw = cw_ref[...]
    cb = cb_ref[...]
    q_all = _conv_silu(q_ref, qp_ref, cw[:, :nqk], cb[:, :nqk], first) * (DQK ** -0.5)
    k_all = _conv_silu(k_ref, kp_ref, cw[:, nqk:], cb[:, nqk:], first)

    li = lax.broadcasted_iota(jnp.int32, (CHUNK, CHUNK), 0)
    si = lax.broadcasted_iota(jnp.int32, (CHUNK, CHUNK), 1)
    causal = si <= li
    strict = jnp.where(li > si, 1.0, 0.0).astype(F32)

    for h in range(NH):
        q = q_all[:, h * DQK:(h + 1) * DQK]
        k = k_all[:, h * DQK:(h + 1) * DQK]
        v = v_ref[:, h * DV:(h + 1) * DV]
        bias_i = bias_ref[h]
        bias_f = bias_ref[NH + h]
        gi_r = gir_ref[h] + bias_i
        lf_r = _log_sigmoid(gfr_ref[h] + bias_f)
        gi_c = gic_ref[h] + bias_i

        lhs = jnp.where(causal, lf_r, 0.0)
        seg = jnp.dot(lhs, strict, preferred_element_type=F32, precision=lax.Precision.HIGHEST)
        b_c = jnp.sum(lhs, axis=-1, keepdims=True)

        m_prev = m_sc[h][:, :1]
        dmat = jnp.where(causal, seg + gi_r, -jnp.inf)
        inter = b_c + m_prev
        m_t = jnp.maximum(inter, jnp.max(dmat, axis=-1, keepdims=True))
        w_intra = jnp.exp(dmat - m_t)
        w_inter = jnp.exp(inter - m_t)

        qb = q.astype(BF16)
        kb = k.astype(BF16)
        s = lax.dot_general(qb, kb, (((1,), (1,)), ((), ())), preferred_element_type=F32) * w_intra
        c_state = c_sc[h]
        n_state = n_sc[h]
        num = jnp.dot(s.astype(BF16), v, preferred_element_type=F32)
        num = num + w_inter * jnp.dot(qb, c_state.astype(BF16), preferred_element_type=F32)
        den = jnp.sum(s, axis=-1, keepdims=True) + w_inter * jnp.sum(q * n_state, axis=-1, keepdims=True)
        hh = num * (1.0 / jnp.maximum(jnp.abs(den), jnp.exp(-m_t)))

        m_new = m_t[CHUNK - 1:CHUNK, :]
        b_last = b_c[CHUNK - 1:CHUNK, :]
        w_k = jnp.exp(b_last - b_c + gi_c - m_new)
        decay = jnp.exp(b_last + m_prev - m_new)
        kw = k * w_k
        c_sc[h] = decay * c_state + jnp.dot(kw.T.astype(BF16), v, preferred_element_type=F32)
        n_sc[h] = decay * n_state + jnp.sum(kw, axis=0, keepdims=True)
        m_sc[h] = jnp.broadcast_to(m_new, (1, CHUNK))

        gate = jax.nn.sigmoid(o_ref[:, h * DV:(h + 1) * DV].astype(F32))
        out = _head_norm(hh, nw_ref[:, h * DV:(h + 1) * DV]) * gate
        y_ref[:, h * DV:(h + 1) * DV] = out.astype(y_ref.dtype)


def _mlstm(ya, gi_row, gf_row, gi_col, if_bias, conv_w, conv_b, norm_w):
    s_len = ya.shape[0]
    nc = s_len // CHUNK
    nqk = NH * DQK
    nv = NH * DV
    prev = lambda c: jnp.maximum(c - 1, 0)
    return pl.pallas_call(
        _mlstm_body,
        grid=(nc,),
        in_specs=[
            pl.BlockSpec(memory_space=pltpu.SMEM),
            pl.BlockSpec((CHUNK, nqk), lambda c: (c, 0)),
            pl.BlockSpec((CHUNK, nqk), lambda c: (prev(c), 0)),
            pl.BlockSpec((CHUNK, nqk), lambda c: (c, 1)),
            pl.BlockSpec((CHUNK, nqk), lambda c: (prev(c), 1)),
            pl.BlockSpec((CHUNK, nv), lambda c: (c, 1)),
            pl.BlockSpec((CHUNK, nv), lambda c: (c, 2)),
            pl.BlockSpec((NH, 1, CHUNK), lambda c: (0, 0, c)),
            pl.BlockSpec((NH, 1, CHUNK), lambda c: (0, 0, c)),
            pl.BlockSpec((NH, CHUNK, 1), lambda c: (0, c, 0)),
            pl.BlockSpec((CONV_W, 2 * nqk), lambda c: (0, 0)),
            pl.BlockSpec((1, 2 * nqk), lambda c: (0, 0)),
            pl.BlockSpec((1, nv), lambda c: (0, 0)),
        ],
        out_specs=pl.BlockSpec((CHUNK, nv), lambda c: (c, 0)),
        out_shape=jax.ShapeDtypeStruct((s_len, nv), BF16),
        scratch_shapes=[
            pltpu.VMEM((NH, DQK, DV), F32),
            pltpu.VMEM((NH, 1, DQK), F32),
            pltpu.VMEM((NH, 1, CHUNK), F32),
        ],
        compiler_params=_params(("arbitrary",)),
        name="mlstm",
    )(if_bias, ya, ya, ya, ya, ya, ya, gi_row, gf_row, gi_col, conv_w, conv_b.reshape(1, -1),
      norm_w.reshape(1, -1))


def _rope(x, cos, sin):
    half = DQK // 2
    x1 = x[:, :half]
    x2 = x[:, half:]
    return jnp.concatenate([x1 * cos - x2 * sin, x2 * cos + x1 * sin], axis=-1)


def _retention_body(lg_ref, pos_ref, inv_ref, q_ref, k_ref, v_ref, g_ref, nw_ref, y_ref, r_sc):
    c = pl.program_id(0)

    @pl.when(c == 0)
    def _():
        r_sc[...] = jnp.zeros_like(r_sc)

    ang = pos_ref[...].astype(F32) * inv_ref[...]
    cos = jnp.cos(ang)
    sin = jnp.sin(ang)
    li = lax.broadcasted_iota(jnp.int32, (CHUNK, CHUNK), 0)
    si = lax.broadcasted_iota(jnp.int32, (CHUNK, CHUNK), 1)
    diff = (li - si).astype(F32)
    pos_c = lax.broadcasted_iota(jnp.int32, (CHUNK, 1), 0).astype(F32)

    for h in range(NH):
        q = _rope(q_ref[:, h * DQK:(h + 1) * DQK].astype(F32), cos, sin)
        k = _rope(k_ref[:, h * DQK:(h + 1) * DQK].astype(F32), cos, sin) * (DQK ** -0.5)
        v = v_ref[:, h * DV:(h + 1) * DV]

        lg = lg_ref[h]
        decay_mask = jnp.where(diff >= 0.0, jnp.exp(lg * jnp.maximum(diff, 0.0)), 0.0)
        q_decay = jnp.exp(lg * (pos_c + 1.0))
        k_decay = jnp.exp(lg * (CHUNK - 1.0 - pos_c))
        chunk_decay = jnp.exp(jnp.full((1, 1), CHUNK, F32) * lg)

        qb = q.astype(BF16)
        s = lax.dot_general(qb, k.astype(BF16), (((1,), (1,)), ((), ())), preferred_element_type=F32) * decay_mask
        r_state = r_sc[h]
        out = jnp.dot(s.astype(BF16), v, preferred_element_type=F32)
        out = out + jnp.dot((q * q_decay).astype(BF16), r_state.astype(BF16), preferred_element_type=F32)
        r_sc[h] = chunk_decay * r_state + jnp.dot((k * k_decay).T.astype(BF16), v, preferred_element_type=F32)

        g = g_ref[:, h * DV:(h + 1) * DV].astype(F32)
        y = _head_norm(out, nw_ref[:, h * DV:(h + 1) * DV]) * (g * jax.nn.sigmoid(g))
        y_ref[:, h * DV:(h + 1) * DV] = y.astype(y_ref.dtype)


def _retention(yb, positions_col, inv_freq, log_gamma, norm_w):
    s_len = yb.shape[0]
    nc = s_len // CHUNK
    nqk = NH * DQK
    nv = NH * DV
    return pl.pallas_call(
        _retention_body,
        grid=(nc,),
        in_specs=[
            pl.BlockSpec(memory_space=pltpu.SMEM),
            pl.BlockSpec((CHUNK, 1), lambda c: (c, 0)),
            pl.BlockSpec((1, DQK // 2), lambda c: (0, 0)),
            pl.BlockSpec((CHUNK, nqk), lambda c: (c, 0)),
            pl.BlockSpec((CHUNK, nqk), lambda c: (c, 1)),
            pl.BlockSpec((CHUNK, nv), lambda c: (c, 1)),
            pl.BlockSpec((CHUNK, nv), lambda c: (c, 2)),
            pl.BlockSpec((1, nv), lambda c: (0, 0)),
        ],
        out_specs=pl.BlockSpec((CHUNK, nv), lambda c: (c, 0)),
        out_shape=jax.ShapeDtypeStruct((s_len, nv), BF16),
        scratch_shapes=[pltpu.VMEM((NH, DQK, DV), F32)],
        compiler_params=_params(("arbitrary",)),
        name="retention",
    )(log_gamma, positions_col, inv_freq, yb, yb, yb, yb, norm_w.reshape(1, -1))


def _ffn(hf, hb, w_gate, w_up, w_down, ln_w, ln_b, *, with_bf16):
    bn = 512
    steps = (D_FF_PAD // bn) * (SEQ // 1024)
    up, wd = _wres_matmul([hb], [w_gate, w_up], (0, 0), _swiglu_epilogue(D_FF, bn), n_out=D_FF_PAD,
                          out_dtype=BF16, bn=bn, side=(w_down, D_FF_PAD // steps, D_FF, D_FF_PAD),
                          name="swiglu_up")
    y = _mm_resid(up, wd, hf, scale=0.5, bk=D_FF_PAD // 4)
    return _layer_norm(y, ln_w, ln_b, with_bf16=with_bf16)


def kernel(x, positions, ffn1_w_gate, ffn1_w_up, ffn1_w_down, ln1_w, ln1_b, w_in, if_bias, conv_w, conv_b,
           norm_m_w, norm_r_w, w_proj_m, w_proj_r, w_out, ln2_w, ln2_b, ffn2_w_gate, ffn2_w_up, ffn2_w_down,
           ln3_w, ln3_b):
    xf = x[0]
    h1f, h1b = _ffn(xf, xf.astype(BF16), ffn1_w_gate, ffn1_w_up, ffn1_w_down, ln1_w[0], ln1_b[0], with_bf16=True)

    n_a = 2 * NH * DQK + 2 * NH * DV
    n_b = 2 * NH * DQK + 2 * NH * DV + 2 * D_MODEL
    w_in_t = jnp.swapaxes(w_in, 1, 2)
    ya = _wres_matmul([h1b], [w_in_t], (0,), _plain_epilogue, n_out=n_a, out_dtype=BF16, bn=1024,
                      transposed=True, name="in_proj_a")
    yb = _wres_matmul([h1b], [w_in_t], (0,), _plain_epilogue, n_out=n_b, out_dtype=BF16, bn=1024,
                      transposed=True, col0=n_a, shift=2 * NH, name="in_proj_b")
    yif = _gates_matmul(h1b, w_in_t, col0=n_a)

    g_rows = yif.T
    gi_row = g_rows[:NH].reshape(NH, 1, SEQ)
    gf_row = g_rows[NH:].reshape(NH, 1, SEQ)
    gi_col = g_rows[:NH].reshape(NH, SEQ, 1)
    y_m = _mlstm(ya, gi_row, gf_row, gi_col, if_bias[0], conv_w[0], conv_b[0], norm_m_w[0])

    inv_freq = (ROPE_BASE ** (-jnp.arange(0, DQK, 2, dtype=F32) / DQK)).reshape(1, DQK // 2)
    log_gamma = jnp.log(1.0 - 2.0 ** (-5.0 - jnp.arange(NH, dtype=F32)))
    y_r = _retention(yb, positions.reshape(SEQ, 1), inv_freq, log_gamma, norm_r_w[0])

    gate_col0 = 2 * NH * DQK + 2 * NH * DV
    bn = 512
    steps = (D_MODEL // bn) * (SEQ // 1024)
    merged, wo = _wres_matmul([y_m, y_r], [w_proj_m, w_proj_r], (0, 1), _merge_epilogue, n_out=D_MODEL,
                              out_dtype=BF16, bn=bn, extras=((yb, gate_col0), (yb, gate_col0 + D_MODEL)),
                              side=(w_out, D_MODEL // steps, D_MODEL, D_MODEL), name="merge")
    y2 = _mm_resid(merged, wo, h1f, scale=1.0, bk=D_MODEL // 2)
    h2f, h2b = _layer_norm(y2, ln2_w[0], ln2_b[0], with_bf16=True)

    (out,) = _ffn(h2f, h2b, ffn2_w_gate, ffn2_w_up, ffn2_w_down, ln3_w[0], ln3_b[0], with_bf16=False)
    return out.reshape(1, SEQ, D_MODEL)
```

```python
import functools
import math

import jax
import jax.numpy as jnp
from jax import lax
from jax.experimental import pallas as pl
from jax.experimental.pallas import tpu as pltpu

D_MODEL = 4096
SEQ = 8192
NH = 4
DQK = 256
DV = 512
CONV_W = 4
D_FF = 11008
D_FF_PAD = 11264
CHUNK = 128
LN_EPS = 1e-5
HEAD_EPS = 1e-6
ALPHA = 2.0 ** 0.25
ROPE_BASE = 10000.0

F32 = jnp.float32
BF16 = jnp.bfloat16

VMEM_LIMIT_BYTES = 56 * 1024 * 1024


def _params(sem):
    return pltpu.CompilerParams(dimension_semantics=sem, vmem_limit_bytes=VMEM_LIMIT_BYTES)


def _wres_body(*refs, n_x, pairs, transposed, shift, n_extra, epilogue, ck, nb, ms, side, rsplit, last_cols):
    n_w = len(pairs)
    pos = 0
    x_refs = refs[pos:pos + n_x]; pos += n_x
    wmain = refs[pos:pos + n_w]; pos += n_w
    wext = refs[pos:pos + n_w] if shift else ()
    pos += n_w if shift else 0
    extra = refs[pos:pos + n_extra]; pos += n_extra
    side_in = refs[pos] if side else None
    pos += 1 if side else 0
    out_ref = refs[pos]; pos += 1
    side_out = refs[pos] if side else None
    pos += 1 if side else 0
    w_sc = [refs[pos + 2 * w:pos + 2 * w + 2] for w in range(n_w)]

    j = pl.program_id(0)
    i = pl.program_id(1)
    row0 = pl.multiple_of(i * ck, ck)

    def cast_piece(cast_slot, p, n_pieces):
        cp = ck // n_pieces
        for w in range(n_w):
            if transposed:
                blk = wmain[w][:, p * cp:(p + 1) * cp]
                if shift:
                    blk = jnp.concatenate([blk[shift:, :], wext[w][:, p * cp:(p + 1) * cp]], axis=0)
                chunk = blk.T
            else:
                chunk = wmain[w][p * cp:(p + 1) * cp, :]
            w_sc[w][cast_slot][pl.ds(pl.multiple_of(row0 + p * cp, cp), cp), :] = chunk.astype(BF16)
        if side:
            s_rows, s_valid = side
            sp = s_rows // n_pieces
            blk = jnp.minimum(j * ms + i, nb * ms - 1)
            ridx = blk * s_rows + p * sp + lax.broadcasted_iota(jnp.int32, (sp, 1), 0)
            piece = side_in[p * sp:(p + 1) * sp, :]
            side_out[p * sp:(p + 1) * sp, :] = jnp.where(ridx < s_valid, piece, 0.0).astype(BF16)

    def step(cast_slot, read_slot, ncols=None):
        if read_slot is None:
            cast_piece(cast_slot, 0, 1)
            return
        bn = out_ref.shape[1]
        rb = out_ref.shape[0] // rsplit
        n_pieces = max(1, rsplit // 2)
        for r in range(rsplit):
            rows = slice(r * rb, (r + 1) * rb)
            if ncols is None:
                dots = [jnp.dot(x_refs[xi][rows, :], w_sc[w][read_slot][...], preferred_element_type=F32)
                        for w, xi in enumerate(pairs)]
                res = epilogue(dots, [e[rows, :] for e in extra], j - 1)
            else:
                dots = [jnp.dot(x_refs[xi][rows, :], w_sc[w][read_slot][:, :ncols], preferred_element_type=F32)
                        for w, xi in enumerate(pairs)]
                res = epilogue(dots, [e[rows, :ncols] for e in extra], j - 1)
                out_ref[rows, ncols:] = jnp.zeros((rb, bn - ncols), out_ref.dtype)
            out_ref[rows, :res.shape[1]] = res.astype(out_ref.dtype)
            if r < n_pieces:
                cast_piece(cast_slot, r, n_pieces)

    even = j % 2 == 0
    full = j < nb if last_cols else j <= nb
    pl.when(j == 0)(lambda: step(0, None))
    pl.when(jnp.logical_and(jnp.logical_and(j >= 1, even), full))(lambda: step(0, 1))
    pl.when(jnp.logical_and(jnp.logical_not(even), full))(lambda: step(1, 0))
    if last_cols:
        pl.when(j == nb)(lambda: step(nb % 2, 1 - nb % 2, last_cols))


def _wres_matmul(xs, ws, pairs, epilogue, *, n_out, out_dtype, bn, n_valid=None, transposed=False, col0=0,
                 shift=0, extras=(), side=None, bm=1024, rsplit=8, name):
    m, k = xs[0].shape
    ms = m // bm
    ck = k // ms
    nb = pl.cdiv(n_out, bn)
    cb0 = col0 // bn
    assert not shift or transposed
    n_valid = n_out if n_valid is None else n_valid
    last_cols = n_valid - (nb - 1) * bn if n_valid < nb * bn else None

    def row_blk(j, i):
        return jnp.where(j > 0, i, 0)

    def w_blk(j):
        return cb0 + jnp.minimum(j, nb - 1)

    in_specs = [pl.BlockSpec((bm, k), lambda j, i: (row_blk(j, i), 0)) for _ in xs]
    if transposed:
        in_specs += [pl.BlockSpec((None, bn, ck), lambda j, i: (0, w_blk(j), i)) for _ in ws]
    else:
        in_specs += [pl.BlockSpec((None, ck, bn), lambda j, i: (0, i, w_blk(j))) for _ in ws]
    args = list(xs) + list(ws)
    if shift:
        eb = bn // shift
        in_specs += [pl.BlockSpec((None, shift, ck), lambda j, i: (0, (w_blk(j) + 1) * eb, i)) for _ in ws]
        args += list(ws)
    for arr, c0 in extras:
        in_specs.append(pl.BlockSpec((bm, bn), lambda j, i, b0=c0 // bn: (row_blk(j, i), b0 + jnp.maximum(j - 1, 0))))
        args.append(arr)
    out_specs = [pl.BlockSpec((bm, bn), lambda j, i: (row_blk(j, i), jnp.maximum(j - 1, 0)))]
    out_shape = [jax.ShapeDtypeStruct((m, n_out), out_dtype)]
    body_side = None
    if side is not None:
        s_arr, s_rows, s_valid, s_out_rows = side
        assert s_out_rows == nb * ms * s_rows
        last_in = s_valid // s_rows - 1
        s_cols = s_arr.shape[2]

        def side_blk(j, i):
            return jnp.minimum(j * ms + i, nb * ms - 1)

        in_specs.append(pl.BlockSpec((None, s_rows, s_cols),
                                     lambda j, i: (0, jnp.minimum(side_blk(j, i), last_in), 0)))
        args.append(s_arr)
        out_specs.append(pl.BlockSpec((s_rows, s_cols), lambda j, i: (side_blk(j, i), 0)))
        out_shape.append(jax.ShapeDtypeStruct((s_out_rows, s_cols), BF16))
        body_side = (s_rows, s_valid)
    body = functools.partial(_wres_body, n_x=len(xs), pairs=tuple(pairs), transposed=transposed, shift=shift,
                             n_extra=len(extras), last_cols=last_cols,
                             epilogue=epilogue, ck=ck, nb=nb, ms=ms, side=body_side, rsplit=rsplit)
    res = pl.pallas_call(
        body,
        grid=(nb + 1, ms),
        in_specs=in_specs,
        out_specs=out_specs,
        out_shape=out_shape,
        scratch_shapes=[pltpu.VMEM((k, bn), BF16) for _ in range(2 * len(ws))],
        compiler_params=_params(("arbitrary", "arbitrary")),
        name=name,
    )(*args)
    return res if side is not None else res[0]


def _swiglu_epilogue(dots, extras, jb):
    g, u = dots
    return g * jax.nn.sigmoid(g) * u


def _plain_epilogue(dots, extras, jb):
    return dots[0]


def _merge_epilogue(dots, extras, jb):
    pm, pr = dots
    ga, gb = extras
    return jax.nn.sigmoid(ga.astype(F32)) * pm + jax.nn.sigmoid(gb.astype(F32)) * pr


def _mm_resid_body(x_ref, w_ref, r_ref, *rest, scale, nk, rsplit, normed):
    if normed:
        mu_ref, rstd_ref, lw_ref, lb_ref, o_ref = rest
    else:
        (o_ref,) = rest
    kk = pl.program_id(2)
    rb = o_ref.shape[0] // rsplit

    def run(first, last):
        for r in range(rsplit):
            rows = slice(r * rb, (r + 1) * rb)
            d = jnp.dot(x_ref[rows, :], w_ref[...], preferred_element_type=F32)
            if not first:
                d = o_ref[rows, :] + d
            if last:
                res = r_ref[rows, :]
                if normed:
                    res = (res - mu_ref[rows, :]) * rstd_ref[rows, :] * lw_ref[...] + lb_ref[...]
                d = ALPHA * res + scale * d
            o_ref[rows, :] = d

    pl.when(kk == 0)(lambda: run(True, nk == 1))
    if nk > 2:
        pl.when(jnp.logical_and(kk > 0, kk < nk - 1))(lambda: run(False, False))
    if nk > 1:
        pl.when(kk == nk - 1)(lambda: run(False, True))


def _mm_resid(xb, w, resid, *, scale, bk, norm=None, bm=2048, bn=1024, rsplit=8):
    m, k = xb.shape
    n = w.shape[1]
    nk = k // bk
    in_specs = [
        pl.BlockSpec((bm, bk), lambda i, j, kk: (i, kk)),
        pl.BlockSpec((bk, bn), lambda i, j, kk: (kk, j)),
        pl.BlockSpec((bm, bn), lambda i, j, kk: (i, j)),
    ]
    args = [xb, w, resid]
    if norm is not None:
        mu, rstd, lw, lb = norm
        in_specs += [
            pl.BlockSpec((bm, 1), lambda i, j, kk: (i, 0)),
            pl.BlockSpec((bm, 1), lambda i, j, kk: (i, 0)),
            pl.BlockSpec((1, bn), lambda i, j, kk: (0, j)),
            pl.BlockSpec((1, bn), lambda i, j, kk: (0, j)),
        ]
        args += [mu, rstd, lw.reshape(1, n), lb.reshape(1, n)]
    return pl.pallas_call(
        functools.partial(_mm_resid_body, scale=scale, nk=nk, rsplit=rsplit, normed=norm is not None),
        grid=(m // bm, n // bn, nk),
        in_specs=in_specs,
        out_specs=pl.BlockSpec((bm, bn), lambda i, j, kk: (i, j)),
        out_shape=jax.ShapeDtypeStruct((m, n), F32),
        compiler_params=_params(("parallel", "parallel", "arbitrary")),
        name="mm_resid",
    )(*args)


def _gates_body(x_ref, wt_ref, o_ref):
    o_ref[...] = lax.dot_general(x_ref[...], wt_ref[...].astype(BF16), (((1,), (1,)), ((), ())),
                                 preferred_element_type=F32)


def _gates_matmul(xb, wt, *, col0, n_cols=8, bm=1024):
    m, k = xb.shape
    return pl.pallas_call(
        _gates_body,
        grid=(m // bm,),
        in_specs=[
            pl.BlockSpec((bm, k), lambda i: (i, 0)),
            pl.BlockSpec((None, n_cols, k), lambda i: (0, col0 // n_cols, 0)),
        ],
        out_specs=pl.BlockSpec((bm, n_cols), lambda i: (i, 0)),
        out_shape=jax.ShapeDtypeStruct((m, n_cols), F32),
        compiler_params=_params(("parallel",)),
        name="gates_mm",
    )(xb, wt)


def _ln_body(y_ref, w_ref, b_ref, o_ref, *stat_refs):
    y = y_ref[...]
    mu = jnp.mean(y, axis=-1, keepdims=True)
    d = y - mu
    var = jnp.mean(d * d, axis=-1, keepdims=True)
    rstd = lax.rsqrt(var + LN_EPS)
    out = d * rstd * w_ref[...] + b_ref[...]
    o_ref[...] = out.astype(o_ref.dtype)
    if stat_refs:
        mu_ref, rstd_ref = stat_refs
        mu_ref[...] = mu
        rstd_ref[...] = rstd


def _layer_norm(y, w, b, *, final, bm=256):
    m, d = y.shape
    row = pl.BlockSpec((bm, d), lambda i: (i, 0))
    stat = pl.BlockSpec((bm, 1), lambda i: (i, 0))
    if final:
        out_specs = [row]
        out_shape = [jax.ShapeDtypeStruct((m, d), F32)]
    else:
        out_specs = [row, stat, stat]
        out_shape = [jax.ShapeDtypeStruct((m, d), BF16), jax.ShapeDtypeStruct((m, 1), F32),
                     jax.ShapeDtypeStruct((m, 1), F32)]
    return pl.pallas_call(
        _ln_body,
        grid=(m // bm,),
        in_specs=[row, pl.BlockSpec((1, d), lambda i: (0, 0)), pl.BlockSpec((1, d), lambda i: (0, 0))],
        out_specs=out_specs,
        out_shape=out_shape,
        compiler_params=_params(("parallel",)),
        name="layer_norm",
    )(y, w.reshape(1, d), b.reshape(1, d))


def _head_norm(hh, w):
    mu = jnp.mean(hh, axis=-1, keepdims=True)
    d = hh - mu
    var = jnp.mean(d * d, axis=-1, keepdims=True)
    return d * lax.rsqrt(var + HEAD_EPS) * w


def _log_sigmoid(x):
    return jnp.minimum(x, 0.0) - jnp.log1p(jnp.exp(-jnp.abs(x)))


def _conv_silu(cur_ref, prev_ref, w, b, first_chunk):
    cur = cur_ref[...].astype(F32)
    prev = jnp.where(first_chunk, 0.0, prev_ref[...].astype(F32))
    row = lax.broadcasted_iota(jnp.int32, (CHUNK, 1), 0)
    acc = cur * w[CONV_W - 1:CONV_W, :] + b
    for j in range(1, CONV_W):
        shifted = jnp.where(row < j, pltpu.roll(prev, j, 0), pltpu.roll(cur, j, 0))
        acc = acc + shifted * w[CONV_W - 1 - j:CONV_W - j, :]
    return acc * jax.nn.sigmoid(acc)


def _mlstm_body(bias_ref, q_ref, qp_ref, k_ref, kp_ref, v_ref, o_ref, gir_ref, gfr_ref, gic_ref,
                cw_ref, cb_ref, nw_ref, y_ref, c_sc, n_sc, m_sc):
    c = pl.program_id(0)

    @pl.when(c == 0)
    def _():
        c_sc[...] = jnp.zeros_like(c_sc)
        n_sc[...] = jnp.zeros_like(n_sc)
        m_sc[...] = jnp.zeros_like(m_sc)

    first = c == 0
    nqk = NH * DQK
    cw = cw_ref[...]
    cb = cb_ref[...]
    q_all = _conv_silu(q_ref, qp_ref, cw[:, :nqk], cb[:, :nqk], first) * (DQK ** -0.5)
    k_all = _conv_silu(k_ref, kp_ref, cw[:, nqk:], cb[:, nqk:], first)

    li = lax.broadcasted_iota(jnp.int32, (CHUNK, CHUNK), 0)
    si = lax.broadcasted_iota(jnp.int32, (CHUNK, CHUNK), 1)
    causal = si <= li
    strict = jnp.where(li > si, 1.0, 0.0).astype(F32)

    for h in range(NH):
        q = q_all[:, h * DQK:(h + 1) * DQK]
        k = k_all[:, h * DQK:(h + 1) * DQK]
        v = v_ref[:, h * DV:(h + 1) * DV]
        bias_i = bias_ref[h]
        bias_f = bias_ref[NH + h]
        gi_r = gir_ref[h] + bias_i
        lf_r = _log_sigmoid(gfr_ref[h] + bias_f)
        gi_c = gic_ref[h] + bias_i

        lhs = jnp.where(causal, lf_r, 0.0)
        seg = jnp.dot(lhs, strict, preferred_element_type=F32, precision=lax.Precision.HIGHEST)
        b_c = jnp.sum(lhs, axis=-1, keepdims=True)

        m_prev = m_sc[h][:, :1]
        dmat = jnp.where(causal, seg + gi_r, -jnp.inf)
        inter = b_c + m_prev
        m_t = jnp.maximum(inter, jnp.max(dmat, axis=-1, keepdims=True))
        w_intra = jnp.exp(dmat - m_t)
        w_inter = jnp.exp(inter - m_t)

        qb = q.astype(BF16)
        kb = k.astype(BF16)
        s = lax.dot_general(qb, kb, (((1,), (1,)), ((), ())), preferred_element_type=F32) * w_intra
        c_state = c_sc[h]
        n_state = n_sc[h]
        num = jnp.dot(s.astype(BF16), v, preferred_element_type=F32)
        num = num + w_inter * jnp.dot(qb, c_state.astype(BF16), preferred_element_type=F32)
        den = jnp.sum(s, axis=-1, keepdims=True) + w_inter * jnp.sum(q * n_state, axis=-1, keepdims=True)
        hh = num * (1.0 / jnp.maximum(jnp.abs(den), jnp.exp(-m_t)))

        m_new = m_t[CHUNK - 1:CHUNK, :]
        b_last = b_c[CHUNK - 1:CHUNK, :]
        w_k = jnp.exp(b_last - b_c + gi_c - m_new)
        decay = jnp.exp(b_last + m_prev - m_new)
        kw = k * w_k
        c_sc[h] = decay * c_state + jnp.dot(kw.T.astype(BF16), v, preferred_element_type=F32)
        n_sc[h] = decay * n_state + jnp.sum(kw, axis=0, keepdims=True)
        m_sc[h] = jnp.broadcast_to(m_new, (1, CHUNK))

        gate = jax.nn.sigmoid(o_ref[:, h * DV:(h + 1) * DV].astype(F32))
        out = _head_norm(hh, nw_ref[:, h * DV:(h + 1) * DV]) * gate
        y_ref[:, h * DV:(h + 1) * DV] = out.astype(y_ref.dtype)


def _mlstm(ya, gi_row, gf_row, gi_col, if_bias, conv_w, conv_b, norm_w):
    s_len = ya.shape[0]
    nc = s_len // CHUNK
    nqk = NH * DQK
    nv = NH * DV
    prev = lambda c: jnp.maximum(c - 1, 0)
    return pl.pallas_call(
        _mlstm_body,
        grid=(nc,),
        in_specs=[
            pl.BlockSpec(memory_space=pltpu.SMEM),
            pl.BlockSpec((CHUNK, nqk), lambda c: (c, 0)),
            pl.BlockSpec((CHUNK, nqk), lambda c: (prev(c), 0)),
            pl.BlockSpec((CHUNK, nqk), lambda c: (c, 1)),
            pl.BlockSpec((CHUNK, nqk), lambda c: (prev(c), 1)),
            pl.BlockSpec((CHUNK, nv), lambda c: (c, 1)),
            pl.BlockSpec((CHUNK, nv), lambda c: (c, 2)),
            pl.BlockSpec((NH, 1, CHUNK), lambda c: (0, 0, c)),
            pl.BlockSpec((NH, 1, CHUNK), lambda c: (0, 0, c)),
            pl.BlockSpec((NH, CHUNK, 1), lambda c: (0, c, 0)),
            pl.BlockSpec((CONV_W, 2 * nqk), lambda c: (0, 0)),
            pl.BlockSpec((1, 2 * nqk), lambda c: (0, 0)),
            pl.BlockSpec((1, nv), lambda c: (0, 0)),
        ],
        out_specs=pl.BlockSpec((CHUNK, nv), lambda c: (c, 0)),
        out_shape=jax.ShapeDtypeStruct((s_len, nv), BF16),
        scratch_shapes=[
            pltpu.VMEM((NH, DQK, DV), F32),
            pltpu.VMEM((NH, 1, DQK), F32),
            pltpu.VMEM((NH, 1, CHUNK), F32),
        ],
        compiler_params=_params(("arbitrary",)),
        name="mlstm",
    )(if_bias, ya, ya, ya, ya, ya, ya, gi_row, gf_row, gi_col, conv_w, conv_b.reshape(1, -1),
      norm_w.reshape(1, -1))


def _rope(x, cos, sin):
    half = DQK // 2
    x1 = x[:, :half]
    x2 = x[:, half:]
    return jnp.concatenate([x1 * cos - x2 * sin, x2 * cos + x1 * sin], axis=-1)


def _retention_body(lg_ref, pos_ref, inv_ref, q_ref, k_ref, v_ref, g_ref, nw_ref, y_ref, r_sc):
    c = pl.program_id(0)

    @pl.when(c == 0)
    def _():
        r_sc[...] = jnp.zeros_like(r_sc)

    ang = pos_ref[...].astype(F32) * inv_ref[...]
    cos = jnp.cos(ang)
    sin = jnp.sin(ang)
    li = lax.broadcasted_iota(jnp.int32, (CHUNK, CHUNK), 0)
    si = lax.broadcasted_iota(jnp.int32, (CHUNK, CHUNK), 1)
    diff = (li - si).astype(F32)
    pos_c = lax.broadcasted_iota(jnp.int32, (CHUNK, 1), 0).astype(F32)

    for h in range(NH):
        q = _rope(q_ref[:, h * DQK:(h + 1) * DQK].astype(F32), cos, sin)
        k = _rope(k_ref[:, h * DQK:(h + 1) * DQK].astype(F32), cos, sin) * (DQK ** -0.5)
        v = v_ref[:, h * DV:(h + 1) * DV]

        lg = lg_ref[h]
        decay_mask = jnp.where(diff >= 0.0, jnp.exp(lg * jnp.maximum(diff, 0.0)), 0.0)
        q_decay = jnp.exp(lg * (pos_c + 1.0))
        k_decay = jnp.exp(lg * (CHUNK - 1.0 - pos_c))
        chunk_decay = jnp.exp(jnp.full((1, 1), CHUNK, F32) * lg)

        qb = q.astype(BF16)
        s = lax.dot_general(qb, k.astype(BF16), (((1,), (1,)), ((), ())), preferred_element_type=F32) * decay_mask
        r_state = r_sc[h]
        out = jnp.dot(s.astype(BF16), v, preferred_element_type=F32)
        out = out + jnp.dot((q * q_decay).astype(BF16), r_state.astype(BF16), preferred_element_type=F32)
        r_sc[h] = chunk_decay * r_state + jnp.dot((k * k_decay).T.astype(BF16), v, preferred_element_type=F32)

        g = g_ref[:, h * DV:(h + 1) * DV].astype(F32)
        y = _head_norm(out, nw_ref[:, h * DV:(h + 1) * DV]) * (g * jax.nn.sigmoid(g))
        y_ref[:, h * DV:(h + 1) * DV] = y.astype(y_ref.dtype)


def _retention(yb, positions_col, inv_freq, log_gamma, norm_w):
    s_len = yb.shape[0]
    nc = s_len // CHUNK
    nqk = NH * DQK
    nv = NH * DV
    return pl.pallas_call(
        _retention_body,
        grid=(nc,),
        in_specs=[
            pl.BlockSpec(memory_space=pltpu.SMEM),
            pl.BlockSpec((CHUNK, 1), lambda c: (c, 0)),
            pl.BlockSpec((1, DQK // 2), lambda c: (0, 0)),
            pl.BlockSpec((CHUNK, nqk), lambda c: (c, 0)),
            pl.BlockSpec((CHUNK, nqk), lambda c: (c, 1)),
            pl.BlockSpec((CHUNK, nv), lambda c: (c, 1)),
            pl.BlockSpec((CHUNK, nv), lambda c: (c, 2)),
            pl.BlockSpec((1, nv), lambda c: (0, 0)),
        ],
        out_specs=pl.BlockSpec((CHUNK, nv), lambda c: (c, 0)),
        out_shape=jax.ShapeDtypeStruct((s_len, nv), BF16),
        scratch_shapes=[pltpu.VMEM((NH, DQK, DV), F32)],
        compiler_params=_params(("arbitrary",)),
        name="retention",
    )(log_gamma, positions_col, inv_freq, yb, yb, yb, yb, norm_w.reshape(1, -1))


def _ffn_pre_norm(hb, resid, norm, w_gate, w_up, w_down):
    bn = 512
    steps = (D_FF_PAD // bn) * (SEQ // 1024)
    up, wd = _wres_matmul([hb], [w_gate, w_up], (0, 0), _swiglu_epilogue, n_out=D_FF_PAD, n_valid=D_FF,
                          out_dtype=BF16, bn=bn, side=(w_down, D_FF_PAD // steps, D_FF, D_FF_PAD),
                          name="swiglu_up")
    return _mm_resid(up, wd, resid, scale=0.5, norm=norm, bk=1024)


def kernel(x, positions, ffn1_w_gate, ffn1_w_up, ffn1_w_down, ln1_w, ln1_b, w_in, if_bias, conv_w, conv_b,
           norm_m_w, norm_r_w, w_proj_m, w_proj_r, w_out, ln2_w, ln2_b, ffn2_w_gate, ffn2_w_up, ffn2_w_down,
           ln3_w, ln3_b):
    xf = x[0]
    y1 = _ffn_pre_norm(xf.astype(BF16), xf, None, ffn1_w_gate, ffn1_w_up, ffn1_w_down)
    h1b, mu1, rstd1 = _layer_norm(y1, ln1_w[0], ln1_b[0], final=False)
    norm1 = (mu1, rstd1, ln1_w[0], ln1_b[0])

    n_a = 2 * NH * DQK + 2 * NH * DV
    n_b = 2 * NH * DQK + 2 * NH * DV + 2 * D_MODEL
    w_in_t = jnp.swapaxes(w_in, 1, 2)
    ya = _wres_matmul([h1b], [w_in_t], (0,), _plain_epilogue, n_out=n_a, out_dtype=BF16, bn=1024,
                      transposed=True, name="in_proj_a")
    yb = _wres_matmul([h1b], [w_in_t], (0,), _plain_epilogue, n_out=n_b, out_dtype=BF16, bn=1024,
                      transposed=True, col0=n_a, shift=2 * NH, name="in_proj_b")
    yif = _gates_matmul(h1b, w_in_t, col0=n_a)

    g_rows = yif.T
    gi_row = g_rows[:NH].reshape(NH, 1, SEQ)
    gf_row = g_rows[NH:].reshape(NH, 1, SEQ)
    gi_col = g_rows[:NH].reshape(NH, SEQ, 1)
    y_m = _mlstm(ya, gi_row, gf_row, gi_col, if_bias[0], conv_w[0], conv_b[0], norm_m_w[0])

    inv_freq = (ROPE_BASE ** (-jnp.arange(0, DQK, 2, dtype=F32) / DQK)).reshape(1, DQK // 2)
    log_gamma = jnp.log(1.0 - 2.0 ** (-5.0 - jnp.arange(NH, dtype=F32)))
    y_r = _retention(yb, positions.reshape(SEQ, 1), inv_freq, log_gamma, norm_r_w[0])

    gate_col0 = 2 * NH * DQK + 2 * NH * DV
    bn = 512
    steps = (D_MODEL // bn) * (SEQ // 1024)
    merged, wo = _wres_matmul([y_m, y_r], [w_proj_m, w_proj_r], (0, 1), _merge_epilogue, n_out=D_MODEL,
                              out_dtype=BF16, bn=bn, extras=((yb, gate_col0), (yb, gate_col0 + D_MODEL)),
                              side=(w_out, D_MODEL // steps, D_MODEL, D_MODEL), name="merge")
    y2 = _mm_resid(merged, wo, y1, scale=1.0, norm=norm1, bk=1024)
    h2b, mu2, rstd2 = _layer_norm(y2, ln2_w[0], ln2_b[0], final=False)
    norm2 = (mu2, rstd2, ln2_w[0], ln2_b[0])

    y3 = _ffn_pre_norm(h2b, y2, norm2, ffn2_w_gate, ffn2_w_up, ffn2_w_down)
    (out,) = _layer_norm(y3, ln3_w[0], ln3_b[0], final=True)
    return out.reshape(1, SEQ, D_MODEL)
```

```python
import functools
import math

import jax
import jax.numpy as jnp
from jax import lax
from jax.experimental import pallas as pl
from jax.experimental.pallas import tpu as pltpu

D_MODEL = 4096
SEQ = 8192
NH = 4
DQK = 256
DV = 512
CONV_W = 4
D_FF = 11008
D_FF_PAD = 11264
CHUNK = 128
LN_EPS = 1e-5
HEAD_EPS = 1e-6
ALPHA = 2.0 ** 0.25
ROPE_BASE = 10000.0

F32 = jnp.float32
BF16 = jnp.bfloat16

VMEM_LIMIT_BYTES = 56 * 1024 * 1024


def _params(sem):
    return pltpu.CompilerParams(dimension_semantics=sem, vmem_limit_bytes=VMEM_LIMIT_BYTES)


def _wres_body(*refs, n_x, pairs, transposed, shift, n_extra, epilogue, ck, nb, ms, side, rsplit, last_cols):
    n_w = len(pairs)
    pos = 0
    x_refs = refs[pos:pos + n_x]; pos += n_x
    wmain = refs[pos:pos + n_w]; pos += n_w
    wext = refs[pos:pos + n_w] if shift else ()
    pos += n_w if shift else 0
    extra = refs[pos:pos + n_extra]; pos += n_extra
    side_in = refs[pos] if side else None
    pos += 1 if side else 0
    out_ref = refs[pos]; pos += 1
    side_out = refs[pos] if side else None
    pos += 1 if side else 0
    w_sc = [refs[pos + 2 * w:pos + 2 * w + 2] for w in range(n_w)]

    j = pl.program_id(0)
    i = pl.program_id(1)
    row0 = pl.multiple_of(i * ck, ck)

    def cast_piece(cast_slot, p, n_pieces):
        cp = ck // n_pieces
        for w in range(n_w):
            if transposed:
                blk = wmain[w][:, p * cp:(p + 1) * cp]
                if shift:
                    blk = jnp.concatenate([blk[shift:, :], wext[w][:, p * cp:(p + 1) * cp]], axis=0)
                chunk = blk.T
            else:
                chunk = wmain[w][p * cp:(p + 1) * cp, :]
            w_sc[w][cast_slot][pl.ds(pl.multiple_of(row0 + p * cp, cp), cp), :] = chunk.astype(BF16)
        if side:
            s_rows, s_valid = side
            sp = s_rows // n_pieces
            blk = jnp.minimum(j * ms + i, nb * ms - 1)
            ridx = blk * s_rows + p * sp + lax.broadcasted_iota(jnp.int32, (sp, 1), 0)
            piece = side_in[p * sp:(p + 1) * sp, :]
            side_out[p * sp:(p + 1) * sp, :] = jnp.where(ridx < s_valid, piece, 0.0).astype(BF16)

    def step(cast_slot, read_slot, ncols=None):
        if read_slot is None:
            cast_piece(cast_slot, 0, 1)
            return
        bn = out_ref.shape[1]
        rb = out_ref.shape[0] // rsplit
        n_pieces = max(1, rsplit // 2)
        for r in range(rsplit):
            rows = slice(r * rb, (r + 1) * rb)
            if ncols is None:
                dots = [jnp.dot(x_refs[xi][rows, :], w_sc[w][read_slot][...], preferred_element_type=F32)
                        for w, xi in enumerate(pairs)]
                res = epilogue(dots, [e[rows, :] for e in extra], j - 1)
            else:
                dots = [jnp.dot(x_refs[xi][rows, :], w_sc[w][read_slot][:, :ncols], preferred_element_type=F32)
                        for w, xi in enumerate(pairs)]
                res = epilogue(dots, [e[rows, :ncols] for e in extra], j - 1)
                out_ref[rows, ncols:] = jnp.zeros((rb, bn - ncols), out_ref.dtype)
            out_ref[rows, :res.shape[1]] = res.astype(out_ref.dtype)
            if r < n_pieces:
                cast_piece(cast_slot, r, n_pieces)

    even = j % 2 == 0
    full = j < nb if last_cols else j <= nb
    pl.when(j == 0)(lambda: step(0, None))
    pl.when(jnp.logical_and(jnp.logical_and(j >= 1, even), full))(lambda: step(0, 1))
    pl.when(jnp.logical_and(jnp.logical_not(even), full))(lambda: step(1, 0))
    if last_cols:
        pl.when(j == nb)(lambda: step(nb % 2, 1 - nb % 2, last_cols))


def _wres_matmul(xs, ws, pairs, epilogue, *, n_out, out_dtype, bn, n_valid=None, transposed=False, col0=0,
                 shift=0, extras=(), side=None, bm=1024, rsplit=8, name):
    m, k = xs[0].shape
    ms = m // bm
    ck = k // ms
    nb = pl.cdiv(n_out, bn)
    cb0 = col0 // bn
    assert not shift or transposed
    n_valid = n_out if n_valid is None else n_valid
    last_cols = n_valid - (nb - 1) * bn if n_valid < nb * bn else None

    def row_blk(j, i):
        return jnp.where(j > 0, i, 0)

    def w_blk(j):
        return cb0 + jnp.minimum(j, nb - 1)

    in_specs = [pl.BlockSpec((bm, k), lambda j, i: (row_blk(j, i), 0)) for _ in xs]
    if transposed:
        in_specs += [pl.BlockSpec((None, bn, ck), lambda j, i: (0, w_blk(j), i)) for _ in ws]
    else:
        in_specs += [pl.BlockSpec((None, ck, bn), lambda j, i: (0, i, w_blk(j))) for _ in ws]
    args = list(xs) + list(ws)
    if shift:
        eb = bn // shift
        in_specs += [pl.BlockSpec((None, shift, ck), lambda j, i: (0, (w_blk(j) + 1) * eb, i)) for _ in ws]
        args += list(ws)
    for arr, c0 in extras:
        in_specs.append(pl.BlockSpec((bm, bn), lambda j, i, b0=c0 // bn: (row_blk(j, i), b0 + jnp.maximum(j - 1, 0))))
        args.append(arr)
    out_specs = [pl.BlockSpec((bm, bn), lambda j, i: (row_blk(j, i), jnp.maximum(j - 1, 0)))]
    out_shape = [jax.ShapeDtypeStruct((m, n_out), out_dtype)]
    body_side = None
    if side is not None:
        s_arr, s_rows, s_valid, s_out_rows = side
        assert s_out_rows == nb * ms * s_rows
        last_in = s_valid // s_rows - 1
        s_cols = s_arr.shape[2]

        def side_blk(j, i):
            return jnp.minimum(j * ms + i, nb * ms - 1)

        in_specs.append(pl.BlockSpec((None, s_rows, s_cols),
                                     lambda j, i: (0, jnp.minimum(side_blk(j, i), last_in), 0)))
        args.append(s_arr)
        out_specs.append(pl.BlockSpec((s_rows, s_cols), lambda j, i: (side_blk(j, i), 0)))
        out_shape.append(jax.ShapeDtypeStruct((s_out_rows, s_cols), BF16))
        body_side = (s_rows, s_valid)
    body = functools.partial(_wres_body, n_x=len(xs), pairs=tuple(pairs), transposed=transposed, shift=shift,
                             n_extra=len(extras), last_cols=last_cols,
                             epilogue=epilogue, ck=ck, nb=nb, ms=ms, side=body_side, rsplit=rsplit)
    res = pl.pallas_call(
        body,
        grid=(nb + 1, ms),
        in_specs=in_specs,
        out_specs=out_specs,
        out_shape=out_shape,
        scratch_shapes=[pltpu.VMEM((k, bn), BF16) for _ in range(2 * len(ws))],
        compiler_params=_params(("arbitrary", "arbitrary")),
        name=name,
    )(*args)
    return res if side is not None else res[0]


def _swiglu_epilogue(dots, extras, jb):
    g, u = dots
    return g * jax.nn.sigmoid(g) * u


def _plain_epilogue(dots, extras, jb):
    return dots[0]


def _merge_epilogue(dots, extras, jb):
    pm, pr = dots
    ga, gb = extras
    return jax.nn.sigmoid(ga.astype(F32)) * pm + jax.nn.sigmoid(gb.astype(F32)) * pr


def _mm_resid_body(x_ref, w_ref, r_ref, *rest, scale, nk, rsplit, normed):
    if normed:
        mu_ref, rstd_ref, lw_ref, lb_ref, o_ref = rest
    else:
        (o_ref,) = rest
    kk = pl.program_id(2)
    rb = o_ref.shape[0] // rsplit

    def run(first, last):
        for r in range(rsplit):
            rows = slice(r * rb, (r + 1) * rb)
            d = jnp.dot(x_ref[rows, :], w_ref[...], preferred_element_type=F32)
            if not first:
                d = o_ref[rows, :] + d
            if last:
                res = r_ref[rows, :]
                if normed:
                    res = (res - mu_ref[rows, :]) * rstd_ref[rows, :] * lw_ref[...] + lb_ref[...]
                d = ALPHA * res + scale * d
            o_ref[rows, :] = d

    pl.when(kk == 0)(lambda: run(True, nk == 1))
    if nk > 2:
        pl.when(jnp.logical_and(kk > 0, kk < nk - 1))(lambda: run(False, False))
    if nk > 1:
        pl.when(kk == nk - 1)(lambda: run(False, True))


def _mm_resid(xb, w, resid, *, scale, bk, norm=None, bm=1024, bn=1024, rsplit=4):
    m, k = xb.shape
    n = w.shape[1]
    nk = k // bk
    in_specs = [
        pl.BlockSpec((bm, bk), lambda i, j, kk: (i, kk)),
        pl.BlockSpec((bk, bn), lambda i, j, kk: (kk, j)),
        pl.BlockSpec((bm, bn), lambda i, j, kk: (i, j)),
    ]
    args = [xb, w, resid]
    if norm is not None:
        mu, rstd, lw, lb = norm
        in_specs += [
            pl.BlockSpec((bm, 1), lambda i, j, kk: (i, 0)),
            pl.BlockSpec((bm, 1), lambda i, j, kk: (i, 0)),
            pl.BlockSpec((1, bn), lambda i, j, kk: (0, j)),
            pl.BlockSpec((1, bn), lambda i, j, kk: (0, j)),
        ]
        args += [mu, rstd, lw.reshape(1, n), lb.reshape(1, n)]
    return pl.pallas_call(
        functools.partial(_mm_resid_body, scale=scale, nk=nk, rsplit=rsplit, normed=norm is not None),
        grid=(m // bm, n // bn, nk),
        in_specs=in_specs,
        out_specs=pl.BlockSpec((bm, bn), lambda i, j, kk: (i, j)),
        out_shape=jax.ShapeDtypeStruct((m, n), F32),
        compiler_params=_params(("parallel", "parallel", "arbitrary")),
        name="mm_resid",
    )(*args)


def _gates_body(x_ref, wt_ref, o_ref):
    o_ref[...] = lax.dot_general(x_ref[...], wt_ref[...].astype(BF16), (((1,), (1,)), ((), ())),
                                 preferred_element_type=F32)


def _gates_matmul(xb, wt, *, col0, n_cols=8, bm=1024):
    m, k = xb.shape
    return pl.pallas_call(
        _gates_body,
        grid=(m // bm,),
        in_specs=[
            pl.BlockSpec((bm, k), lambda i: (i, 0)),
            pl.BlockSpec((None, n_cols, k), lambda i: (0, col0 // n_cols, 0)),
        ],
        out_specs=pl.BlockSpec((bm, n_cols), lambda i: (i, 0)),
        out_shape=jax.ShapeDtypeStruct((m, n_cols), F32),
        compiler_params=_params(("parallel",)),
        name="gates_mm",
    )(xb, wt)


def _ln_body(y_ref, w_ref, b_ref, o_ref, *stat_refs):
    y = y_ref[...]
    mu = jnp.mean(y, axis=-1, keepdims=True)
    d = y - mu
    var = jnp.mean(d * d, axis=-1, keepdims=True)
    rstd = lax.rsqrt(var + LN_EPS)
    out = d * rstd * w_ref[...] + b_ref[...]
    o_ref[...] = out.astype(o_ref.dtype)
    if stat_refs:
        mu_ref, rstd_ref = stat_refs
        mu_ref[...] = mu
        rstd_ref[...] = rstd


def _layer_norm(y, w, b, *, final, bm=256):
    m, d = y.shape
    row = pl.BlockSpec((bm, d), lambda i: (i, 0))
    stat = pl.BlockSpec((bm, 1), lambda i: (i, 0))
    if final:
        out_specs = [row]
        out_shape = [jax.ShapeDtypeStruct((m, d), F32)]
    else:
        out_specs = [row, stat, stat]
        out_shape = [jax.ShapeDtypeStruct((m, d), BF16), jax.ShapeDtypeStruct((m, 1), F32),
                     jax.ShapeDtypeStruct((m, 1), F32)]
    return pl.pallas_call(
        _ln_body,
        grid=(m // bm,),
        in_specs=[row, pl.BlockSpec((1, d), lambda i: (0, 0)), pl.BlockSpec((1, d), lambda i: (0, 0))],
        out_specs=out_specs,
        out_shape=out_shape,
        compiler_params=_params(("parallel",)),
        name="layer_norm",
    )(y, w.reshape(1, d), b.reshape(1, d))


def _head_norm(hh, w):
    mu = jnp.mean(hh, axis=-1, keepdims=True)
    d = hh - mu
    var = jnp.mean(d * d, axis=-1, keepdims=True)
    return d * lax.rsqrt(var + HEAD_EPS) * w


def _log_sigmoid(x):
    return jnp.minimum(x, 0.0) - jnp.log1p(jnp.exp(-jnp.abs(x)))


def _conv_silu(cur_ref, prev_ref, w, b, first_chunk):
    cur = cur_ref[...].astype(F32)
    prev = jnp.where(first_chunk, 0.0, prev_ref[...].astype(F32))
    row = lax.broadcasted_iota(jnp.int32, (CHUNK, 1), 0)
    acc = cur * w[CONV_W - 1:CONV_W, :] + b
    for j in range(1, CONV_W):
        shifted = jnp.where(row < j, pltpu.roll(prev, j, 0), pltpu.roll(cur, j, 0))
        acc = acc + shifted * w[CONV_W - 1 - j:CONV_W - j, :]
    return acc * jax.nn.sigmoid(acc)


def _mlstm_body(bias_ref, q_ref, qp_ref, k_ref, kp_ref, v_ref, o_ref, gir_ref, gfr_ref, gic_ref,
                cw_ref, cb_ref, nw_ref, y_ref, c_sc, n_sc, m_sc):
    c = pl.program_id(0)

    @pl.when(c == 0)
    def _():
        c_sc[...] = jnp.zeros_like(c_sc)
        n_sc[...] = jnp.zeros_like(n_sc)
        m_sc[...] = jnp.zeros_like(m_sc)

    first = c == 0
    nqk = NH * DQK
    cw = cw_ref[...]
    cb = cb_ref[...]
    q_all = _conv_silu(q_ref, qp_ref, cw[:, :nqk], cb[:, :nqk], first) * (DQK ** -0.5)
    k_all = _conv_silu(k_ref, kp_ref, cw[:, nqk:], cb[:, nqk:], first)

    li = lax.broadcasted_iota(jnp.int32, (CHUNK, CHUNK), 0)
    si = lax.broadcasted_iota(jnp.int32, (CHUNK, CHUNK), 1)
    causal = si <= li
    strict = jnp.where(li > si, 1.0, 0.0).astype(F32)

    for h in range(NH):
        q = q_all[:, h * DQK:(h + 1) * DQK]
        k = k_all[:, h * DQK:(h + 1) * DQK]
        v = v_ref[:, h * DV:(h + 1) * DV]
        bias_i = bias_ref[h]
        bias_f = bias_ref[NH + h]
        gi_r = gir_ref[h] + bias_i
        lf_r = _log_sigmoid(gfr_ref[h] + bias_f)
        gi_c = gic_ref[h] + bias_i

        lhs = jnp.where(causal, lf_r, 0.0)
        seg = jnp.dot(lhs, strict, preferred_element_type=F32, precision=lax.Precision.HIGHEST)
        b_c = jnp.sum(lhs, axis=-1, keepdims=True)

        m_prev = m_sc[h][:, :1]
        dmat = jnp.where(causal, seg + gi_r, -jnp.inf)
        inter = b_c + m_prev
        m_t = jnp.maximum(inter, jnp.max(dmat, axis=-1, keepdims=True))
        w_intra = jnp.exp(dmat - m_t)
        w_inter = jnp.exp(inter - m_t)

        qb = q.astype(BF16)
        kb = k.astype(BF16)
        s = lax.dot_general(qb, kb, (((1,), (1,)), ((), ())), preferred_element_type=F32) * w_intra
        c_state = c_sc[h]
        n_state = n_sc[h]
        num = jnp.dot(s.astype(BF16), v, preferred_element_type=F32)
        num = num + w_inter * jnp.dot(qb, c_state.astype(BF16), preferred_element_type=F32)
        den = jnp.sum(s, axis=-1, keepdims=True) + w_inter * jnp.sum(q * n_state, axis=-1, keepdims=True)
        hh = num * (1.0 / jnp.maximum(jnp.abs(den), jnp.exp(-m_t)))

        m_new = m_t[CHUNK - 1:CHUNK, :]
        b_last = b_c[CHUNK - 1:CHUNK, :]
        w_k = jnp.exp(b_last - b_c + gi_c - m_new)
        decay = jnp.exp(b_last + m_prev - m_new)
        kw = k * w_k
        c_sc[h] = decay * c_state + jnp.dot(kw.T.astype(BF16), v, preferred_element_type=F32)
        n_sc[h] = decay * n_state + jnp.sum(kw, axis=0, keepdims=True)
        m_sc[h] = jnp.broadcast_to(m_new, (1, CHUNK))

        gate = jax.nn.sigmoid(o_ref[:, h * DV:(h + 1) * DV].astype(F32))
        out = _head_norm(hh, nw_ref[:, h * DV:(h + 1) * DV]) * gate
        y_ref[:, h * DV:(h + 1) * DV] = out.astype(y_ref.dtype)


def _mlstm(ya, gi_row, gf_row, gi_col, if_bias, conv_w, conv_b, norm_w):
    s_len = ya.shape[0]
    nc = s_len // CHUNK
    nqk = NH * DQK
    nv = NH * DV
    prev = lambda c: jnp.maximum(c - 1, 0)
    return pl.pallas_call(
        _mlstm_body,
        grid=(nc,),
        in_specs=[
            pl.BlockSpec(memory_space=pltpu.SMEM),
            pl.BlockSpec((CHUNK, nqk), lambda c: (c, 0)),
            pl.BlockSpec((CHUNK, nqk), lambda c: (prev(c), 0)),
            pl.BlockSpec((CHUNK, nqk), lambda c: (c, 1)),
            pl.BlockSpec((CHUNK, nqk), lambda c: (prev(c), 1)),
            pl.BlockSpec((CHUNK, nv), lambda c: (c, 1)),
            pl.BlockSpec((CHUNK, nv), lambda c: (c, 2)),
            pl.BlockSpec((NH, 1, CHUNK), lambda c: (0, 0, c)),
            pl.BlockSpec((NH, 1, CHUNK), lambda c: (0, 0, c)),
            pl.BlockSpec((NH, CHUNK, 1), lambda c: (0, c, 0)),
            pl.BlockSpec((CONV_W, 2 * nqk), lambda c: (0, 0)),
            pl.BlockSpec((1, 2 * nqk), lambda c: (0, 0)),
            pl.BlockSpec((1, nv), lambda c: (0, 0)),
        ],
        out_specs=pl.BlockSpec((CHUNK, nv), lambda c: (c, 0)),
        out_shape=jax.ShapeDtypeStruct((s_len, nv), BF16),
        scratch_shapes=[
            pltpu.VMEM((NH, DQK, DV), F32),
            pltpu.VMEM((NH, 1, DQK), F32),
            pltpu.VMEM((NH, 1, CHUNK), F32),
        ],
        compiler_params=_params(("arbitrary",)),
        name="mlstm",
    )(if_bias, ya, ya, ya, ya, ya, ya, gi_row, gf_row, gi_col, conv_w, conv_b.reshape(1, -1),
      norm_w.reshape(1, -1))


def _rope(x, cos, sin):
    half = DQK // 2
    x1 = x[:, :half]
    x2 = x[:, half:]
    return jnp.concatenate([x1 * cos - x2 * sin, x2 * cos + x1 * sin], axis=-1)


def _retention_body(lg_ref, pos_ref, inv_ref, q_ref, k_ref, v_ref, g_ref, nw_ref, y_ref, r_sc):
    c = pl.program_id(0)

    @pl.when(c == 0)
    def _():
        r_sc[...] = jnp.zeros_like(r_sc)

    ang = pos_ref[...].astype(F32) * inv_ref[...]
    cos = jnp.cos(ang)
    sin = jnp.sin(ang)
    li = lax.broadcasted_iota(jnp.int32, (CHUNK, CHUNK), 0)
    si = lax.broadcasted_iota(jnp.int32, (CHUNK, CHUNK), 1)
    diff = (li - si).astype(F32)
    pos_c = lax.broadcasted_iota(jnp.int32, (CHUNK, 1), 0).astype(F32)

    for h in range(NH):
        q = _rope(q_ref[:, h * DQK:(h + 1) * DQK].astype(F32), cos, sin)
        k = _rope(k_ref[:, h * DQK:(h + 1) * DQK].astype(F32), cos, sin) * (DQK ** -0.5)
        v = v_ref[:, h * DV:(h + 1) * DV]

        lg = lg_ref[h]
        decay_mask = jnp.where(diff >= 0.0, jnp.exp(lg * jnp.maximum(diff, 0.0)), 0.0)
        q_decay = jnp.exp(lg * (pos_c + 1.0))
        k_decay = jnp.exp(lg * (CHUNK - 1.0 - pos_c))
        chunk_decay = jnp.exp(jnp.full((1, 1), CHUNK, F32) * lg)

        qb = q.astype(BF16)
        s = lax.dot_general(qb, k.astype(BF16), (((1,), (1,)), ((), ())), preferred_element_type=F32) * decay_mask
        r_state = r_sc[h]
        out = jnp.dot(s.astype(BF16), v, preferred_element_type=F32)
        out = out + jnp.dot((q * q_decay).astype(BF16), r_state.astype(BF16), preferred_element_type=F32)
        r_sc[h] = chunk_decay * r_state + jnp.dot((k * k_decay).T.astype(BF16), v, preferred_element_type=F32)

        g = g_ref[:, h * DV:(h + 1) * DV].astype(F32)
        y = _head_norm(out, nw_ref[:, h * DV:(h + 1) * DV]) * (g * jax.nn.sigmoid(g))
        y_ref[:, h * DV:(h + 1) * DV] = y.astype(y_ref.dtype)


def _retention(yb, positions_col, inv_freq, log_gamma, norm_w):
    s_len = yb.shape[0]
    nc = s_len // CHUNK
    nqk = NH * DQK
    nv = NH * DV
    return pl.pallas_call(
        _retention_body,
        grid=(nc,),
        in_specs=[
            pl.BlockSpec(memory_space=pltpu.SMEM),
            pl.BlockSpec((CHUNK, 1), lambda c: (c, 0)),
            pl.BlockSpec((1, DQK // 2), lambda c: (0, 0)),
            pl.BlockSpec((CHUNK, nqk), lambda c: (c, 0)),
            pl.BlockSpec((CHUNK, nqk), lambda c: (c, 1)),
            pl.BlockSpec((CHUNK, nv), lambda c: (c, 1)),
            pl.BlockSpec((CHUNK, nv), lambda c: (c, 2)),
            pl.BlockSpec((1, nv), lambda c: (0, 0)),
        ],
        out_specs=pl.BlockSpec((CHUNK, nv), lambda c: (c, 0)),
        out_shape=jax.ShapeDtypeStruct((s_len, nv), BF16),
        scratch_shapes=[pltpu.VMEM((NH, DQK, DV), F32)],
        compiler_params=_params(("arbitrary",)),
        name="retention",
    )(log_gamma, positions_col, inv_freq, yb, yb, yb, yb, norm_w.reshape(1, -1))


def _ffn_pre_norm(hb, resid, norm, w_gate, w_up, w_down):
    bn = 512
    steps = (D_FF_PAD // bn) * (SEQ // 1024)
    up, wd = _wres_matmul([hb], [w_gate, w_up], (0, 0), _swiglu_epilogue, n_out=D_FF_PAD, n_valid=D_FF,
                          out_dtype=BF16, bn=bn, side=(w_down, D_FF_PAD // steps, D_FF, D_FF_PAD),
                          name="swiglu_up")
    return _mm_resid(up, wd, resid, scale=0.5, norm=norm, bk=D_FF_PAD // 4)


def kernel(x, positions, ffn1_w_gate, ffn1_w_up, ffn1_w_down, ln1_w, ln1_b, w_in, if_bias, conv_w, conv_b,
           norm_m_w, norm_r_w, w_proj_m, w_proj_r, w_out, ln2_w, ln2_b, ffn2_w_gate, ffn2_w_up, ffn2_w_down,
           ln3_w, ln3_b):
    xf = x[0]
    y1 = _ffn_pre_norm(xf.astype(BF16), xf, None, ffn1_w_gate, ffn1_w_up, ffn1_w_down)
    h1b, mu1, rstd1 = _layer_norm(y1, ln1_w[0], ln1_b[0], final=False)
    norm1 = (mu1, rstd1, ln1_w[0], ln1_b[0])

    n_a = 2 * NH * DQK + 2 * NH * DV
    n_b = 2 * NH * DQK + 2 * NH * DV + 2 * D_MODEL
    w_in_t = jnp.swapaxes(w_in, 1, 2)
    ya = _wres_matmul([h1b], [w_in_t], (0,), _plain_epilogue, n_out=n_a, out_dtype=BF16, bn=1024,
                      transposed=True, name="in_proj_a")
    yb = _wres_matmul([h1b], [w_in_t], (0,), _plain_epilogue, n_out=n_b, out_dtype=BF16, bn=1024,
                      transposed=True, col0=n_a, shift=2 * NH, name="in_proj_b")
    yif = _gates_matmul(h1b, w_in_t, col0=n_a)

    g_rows = yif.T
    gi_row = g_rows[:NH].reshape(NH, 1, SEQ)
    gf_row = g_rows[NH:].reshape(NH, 1, SEQ)
    gi_col = g_rows[:NH].reshape(NH, SEQ, 1)
    y_m = _mlstm(ya, gi_row, gf_row, gi_col, if_bias[0], conv_w[0], conv_b[0], norm_m_w[0])

    inv_freq = (ROPE_BASE ** (-jnp.arange(0, DQK, 2, dtype=F32) / DQK)).reshape(1, DQK // 2)
    log_gamma = jnp.log(1.0 - 2.0 ** (-5.0 - jnp.arange(NH, dtype=F32)))
    y_r = _retention(yb, positions.reshape(SEQ, 1), inv_freq, log_gamma, norm_r_w[0])

    gate_col0 = 2 * NH * DQK + 2 * NH * DV
    bn = 512
    steps = (D_MODEL // bn) * (SEQ // 1024)
    merged, wo = _wres_matmul([y_m, y_r], [w_proj_m, w_proj_r], (0, 1), _merge_epilogue, n_out=D_MODEL,
                              out_dtype=BF16, bn=bn, extras=((yb, gate_col0), (yb, gate_col0 + D_MODEL)),
                              side=(w_out, D_MODEL // steps, D_MODEL, D_MODEL), name="merge")
    y2 = _mm_resid(merged, wo, y1, scale=1.0, norm=norm1, bk=D_MODEL // 2)
    h2b, mu2, rstd2 = _layer_norm(y2, ln2_w[0], ln2_b[0], final=False)
    norm2 = (mu2, rstd2, ln2_w[0], ln2_b[0])

    y3 = _ffn_pre_norm(h2b, y2, norm2, ffn2_w_gate, ffn2_w_up, ffn2_w_down)
    (out,) = _layer_norm(y3, ln3_w[0], ln3_b[0], final=True)
    return out.reshape(1, SEQ, D_MODEL)
```

```python
import functools
import math

import jax
import jax.numpy as jnp
from jax import lax
from jax.experimental import pallas as pl
from jax.experimental.pallas import tpu as pltpu

D_MODEL = 4096
SEQ = 8192
NH = 4
DQK = 256
DV = 512
CONV_W = 4
D_FF = 11008
D_FF_PAD = 11264
CHUNK = 128
LN_EPS = 1e-5
HEAD_EPS = 1e-6
ALPHA = 2.0 ** 0.25
ROPE_BASE = 10000.0

F32 = jnp.float32
BF16 = jnp.bfloat16

VMEM_LIMIT_BYTES = 56 * 1024 * 1024


def _params(sem):
    return pltpu.CompilerParams(dimension_semantics=sem, vmem_limit_bytes=VMEM_LIMIT_BYTES)


def _wres_body(*refs, n_x, pairs, transposed, shift, n_extra, epilogue, ck, nb, ms, side, rsplit, last_cols):
    n_w = len(pairs)
    pos = 0
    x_refs = refs[pos:pos + n_x]; pos += n_x
    wmain = refs[pos:pos + n_w]; pos += n_w
    wext = refs[pos:pos + n_w] if shift else ()
    pos += n_w if shift else 0
    extra = refs[pos:pos + n_extra]; pos += n_extra
    side_in = refs[pos] if side else None
    pos += 1 if side else 0
    out_ref = refs[pos]; pos += 1
    side_out = refs[pos] if side else None
    pos += 1 if side else 0
    w_sc = [refs[pos + 2 * w:pos + 2 * w + 2] for w in range(n_w)]

    j = pl.program_id(0)
    i = pl.program_id(1)
    row0 = pl.multiple_of(i * ck, ck)

    def cast_piece(cast_slot, p, n_pieces):
        cp = ck // n_pieces
        for w in range(n_w):
            if transposed:
                blk = wmain[w][:, p * cp:(p + 1) * cp]
                if shift:
                    blk = jnp.concatenate([blk[shift:, :], wext[w][:, p * cp:(p + 1) * cp]], axis=0)
                chunk = blk.T
            else:
                chunk = wmain[w][p * cp:(p + 1) * cp, :]
            w_sc[w][cast_slot][pl.ds(pl.multiple_of(row0 + p * cp, cp), cp), :] = chunk.astype(BF16)
        if side:
            s_rows, s_valid = side
            sp = s_rows // n_pieces
            blk = jnp.minimum(j * ms + i, nb * ms - 1)
            ridx = blk * s_rows + p * sp + lax.broadcasted_iota(jnp.int32, (sp, 1), 0)
            piece = side_in[p * sp:(p + 1) * sp, :]
            side_out[p * sp:(p + 1) * sp, :] = jnp.where(ridx < s_valid, piece, 0.0).astype(BF16)

    def step(cast_slot, read_slot, ncols=None):
        if read_slot is None:
            cast_piece(cast_slot, 0, 1)
            return
        bn = out_ref.shape[1]
        rb = out_ref.shape[0] // rsplit
        n_pieces = max(1, rsplit // 2)
        for r in range(rsplit):
            rows = slice(r * rb, (r + 1) * rb)
            if ncols is None:
                dots = [jnp.dot(x_refs[xi][rows, :], w_sc[w][read_slot][...], preferred_element_type=F32)
                        for w, xi in enumerate(pairs)]
                res = epilogue(dots, [e[rows, :] for e in extra], j - 1)
            else:
                dots = [jnp.dot(x_refs[xi][rows, :], w_sc[w][read_slot][:, :ncols], preferred_element_type=F32)
                        for w, xi in enumerate(pairs)]
                res = epilogue(dots, [e[rows, :ncols] for e in extra], j - 1)
                out_ref[rows, ncols:] = jnp.zeros((rb, bn - ncols), out_ref.dtype)
            out_ref[rows, :res.shape[1]] = res.astype(out_ref.dtype)
            if r < n_pieces:
                cast_piece(cast_slot, r, n_pieces)

    even = j % 2 == 0
    full = j < nb if last_cols else j <= nb
    pl.when(j == 0)(lambda: step(0, None))
    pl.when(jnp.logical_and(jnp.logical_and(j >= 1, even), full))(lambda: step(0, 1))
    pl.when(jnp.logical_and(jnp.logical_not(even), full))(lambda: step(1, 0))
    if last_cols:
        pl.when(j == nb)(lambda: step(nb % 2, 1 - nb % 2, last_cols))


def _wres_matmul(xs, ws, pairs, epilogue, *, n_out, out_dtype, bn, n_valid=None, transposed=False, col0=0,
                 shift=0, extras=(), side=None, bm=1024, rsplit=8, name):
    m, k = xs[0].shape
    ms = m // bm
    ck = k // ms
    nb = pl.cdiv(n_out, bn)
    cb0 = col0 // bn
    assert not shift or transposed
    n_valid = n_out if n_valid is None else n_valid
    last_cols = n_valid - (nb - 1) * bn if n_valid < nb * bn else None

    def row_blk(j, i):
        return jnp.where(j > 0, i, 0)

    def w_blk(j):
        return cb0 + jnp.minimum(j, nb - 1)

    in_specs = [pl.BlockSpec((bm, k), lambda j, i: (row_blk(j, i), 0)) for _ in xs]
    if transposed:
        in_specs += [pl.BlockSpec((None, bn, ck), lambda j, i: (0, w_blk(j), i)) for _ in ws]
    else:
        in_specs += [pl.BlockSpec((None, ck, bn), lambda j, i: (0, i, w_blk(j))) for _ in ws]
    args = list(xs) + list(ws)
    if shift:
        eb = bn // shift
        in_specs += [pl.BlockSpec((None, shift, ck), lambda j, i: (0, (w_blk(j) + 1) * eb, i)) for _ in ws]
        args += list(ws)
    for arr, c0 in extras:
        in_specs.append(pl.BlockSpec((bm, bn), lambda j, i, b0=c0 // bn: (row_blk(j, i), b0 + jnp.maximum(j - 1, 0))))
        args.append(arr)
    out_specs = [pl.BlockSpec((bm, bn), lambda j, i: (row_blk(j, i), jnp.maximum(j - 1, 0)))]
    out_shape = [jax.ShapeDtypeStruct((m, n_out), out_dtype)]
    body_side = None
    if side is not None:
        s_arr, s_rows, s_valid, s_out_rows = side
        assert s_out_rows == nb * ms * s_rows
        last_in = s_valid // s_rows - 1
        s_cols = s_arr.shape[2]

        def side_blk(j, i):
            return jnp.minimum(j * ms + i, nb * ms - 1)

        in_specs.append(pl.BlockSpec((None, s_rows, s_cols),
                                     lambda j, i: (0, jnp.minimum(side_blk(j, i), last_in), 0)))
        args.append(s_arr)
        out_specs.append(pl.BlockSpec((s_rows, s_cols), lambda j, i: (side_blk(j, i), 0)))
        out_shape.append(jax.ShapeDtypeStruct((s_out_rows, s_cols), BF16))
        body_side = (s_rows, s_valid)
    body = functools.partial(_wres_body, n_x=len(xs), pairs=tuple(pairs), transposed=transposed, shift=shift,
                             n_extra=len(extras), last_cols=last_cols,
                             epilogue=epilogue, ck=ck, nb=nb, ms=ms, side=body_side, rsplit=rsplit)
    res = pl.pallas_call(
        body,
        grid=(nb + 1, ms),
        in_specs=in_specs,
        out_specs=out_specs,
        out_shape=out_shape,
        scratch_shapes=[pltpu.VMEM((k, bn), BF16) for _ in range(2 * len(ws))],
        compiler_params=_params(("arbitrary", "arbitrary")),
        name=name,
    )(*args)
    return res if side is not None else res[0]


def _swiglu_epilogue(dots, extras, jb):
    g, u = dots
    return g * jax.nn.sigmoid(g) * u


def _plain_epilogue(dots, extras, jb):
    return dots[0]


def _merge_epilogue(dots, extras, jb):
    pm, pr = dots
    ga, gb = extras
    return jax.nn.sigmoid(ga.astype(F32)) * pm + jax.nn.sigmoid(gb.astype(F32)) * pr


def _mm_resid_body(x_ref, w_ref, r_ref, *rest, scale, nk, rsplit, normed):
    if normed:
        mu_ref, rstd_ref, lw_ref, lb_ref, o_ref = rest
    else:
        (o_ref,) = rest
    kk = pl.program_id(2)
    rb = o_ref.shape[0] // rsplit

    def run(first, last):
        for r in range(rsplit):
            rows = slice(r * rb, (r + 1) * rb)
            d = jnp.dot(x_ref[rows, :], w_ref[...], preferred_element_type=F32)
            if not first:
                d = o_ref[rows, :] + d
            if last:
                res = r_ref[rows, :]
                if normed:
                    res = (res - mu_ref[rows, :]) * rstd_ref[rows, :] * lw_ref[...] + lb_ref[...]
                d = ALPHA * res + scale * d
            o_ref[rows, :] = d

    pl.when(kk == 0)(lambda: run(True, nk == 1))
    if nk > 2:
        pl.when(jnp.logical_and(kk > 0, kk < nk - 1))(lambda: run(False, False))
    if nk > 1:
        pl.when(kk == nk - 1)(lambda: run(False, True))


def _mm_resid(xb, w, resid, *, scale, bk, norm=None, bm=1024, bn=1024, rsplit=4):
    m, k = xb.shape
    n = w.shape[1]
    nk = k // bk
    in_specs = [
        pl.BlockSpec((bm, bk), lambda i, j, kk: (i, kk)),
        pl.BlockSpec((bk, bn), lambda i, j, kk: (kk, j)),
        pl.BlockSpec((bm, bn), lambda i, j, kk: (i, j)),
    ]
    args = [xb, w, resid]
    if norm is not None:
        mu, rstd, lw, lb = norm
        in_specs += [
            pl.BlockSpec((bm, 1), lambda i, j, kk: (i, 0)),
            pl.BlockSpec((bm, 1), lambda i, j, kk: (i, 0)),
            pl.BlockSpec((1, bn), lambda i, j, kk: (0, j)),
            pl.BlockSpec((1, bn), lambda i, j, kk: (0, j)),
        ]
        args += [mu, rstd, lw.reshape(1, n), lb.reshape(1, n)]
    return pl.pallas_call(
        functools.partial(_mm_resid_body, scale=scale, nk=nk, rsplit=rsplit, normed=norm is not None),
        grid=(m // bm, n // bn, nk),
        in_specs=in_specs,
        out_specs=pl.BlockSpec((bm, bn), lambda i, j, kk: (i, j)),
        out_shape=jax.ShapeDtypeStruct((m, n), F32),
        compiler_params=_params(("parallel", "parallel", "arbitrary")),
        name="mm_resid",
    )(*args)


def _gates_body(x_ref, wt_ref, o_ref):
    o_ref[...] = lax.dot_general(x_ref[...], wt_ref[...].astype(BF16), (((1,), (1,)), ((), ())),
                                 preferred_element_type=F32)


def _gates_matmul(xb, wt, *, col0, n_cols=8, bm=1024):
    m, k = xb.shape
    return pl.pallas_call(
        _gates_body,
        grid=(m // bm,),
        in_specs=[
            pl.BlockSpec((bm, k), lambda i: (i, 0)),
            pl.BlockSpec((None, n_cols, k), lambda i: (0, col0 // n_cols, 0)),
        ],
        out_specs=pl.BlockSpec((bm, n_cols), lambda i: (i, 0)),
        out_shape=jax.ShapeDtypeStruct((m, n_cols), F32),
        compiler_params=_params(("parallel",)),
        name="gates_mm",
    )(xb, wt)


LN_SUB_ROWS = 16


def _ln_body(y_ref, w_ref, b_ref, o_ref, *stat_refs):
    def sub(r, carry):
        rows = pl.ds(pl.multiple_of(r * LN_SUB_ROWS, LN_SUB_ROWS), LN_SUB_ROWS)
        y = y_ref[rows, :]
        mu = jnp.mean(y, axis=-1, keepdims=True)
        d = y - mu
        var = jnp.mean(d * d, axis=-1, keepdims=True)
        rstd = lax.rsqrt(var + LN_EPS)
        out = d * rstd * w_ref[...] + b_ref[...]
        o_ref[rows, :] = out.astype(o_ref.dtype)
        if stat_refs:
            mu_ref, rstd_ref = stat_refs
            mu_ref[rows, :] = mu
            rstd_ref[rows, :] = rstd
        return carry

    lax.fori_loop(0, y_ref.shape[0] // LN_SUB_ROWS, sub, 0, unroll=8)


def _layer_norm(y, w, b, *, final, bm=512):
    m, d = y.shape
    row = pl.BlockSpec((bm, d), lambda i: (i, 0))
    stat = pl.BlockSpec((bm, 1), lambda i: (i, 0))
    if final:
        out_specs = [row]
        out_shape = [jax.ShapeDtypeStruct((m, d), F32)]
    else:
        out_specs = [row, stat, stat]
        out_shape = [jax.ShapeDtypeStruct((m, d), BF16), jax.ShapeDtypeStruct((m, 1), F32),
                     jax.ShapeDtypeStruct((m, 1), F32)]
    return pl.pallas_call(
        _ln_body,
        grid=(m // bm,),
        in_specs=[row, pl.BlockSpec((1, d), lambda i: (0, 0)), pl.BlockSpec((1, d), lambda i: (0, 0))],
        out_specs=out_specs,
        out_shape=out_shape,
        compiler_params=_params(("parallel",)),
        name="layer_norm",
    )(y, w.reshape(1, d), b.reshape(1, d))


def _head_norm(hh, w):
    mu = jnp.mean(hh, axis=-1, keepdims=True)
    d = hh - mu
    var = jnp.mean(d * d, axis=-1, keepdims=True)
    return d * lax.rsqrt(var + HEAD_EPS) * w


def _log_sigmoid(x):
    return jnp.minimum(x, 0.0) - jnp.log1p(jnp.exp(-jnp.abs(x)))


def _conv_silu(cur_ref, prev_ref, w, b, first_chunk):
    cur = cur_ref[...].astype(F32)
    prev = jnp.where(first_chunk, 0.0, prev_ref[...].astype(F32))
    row = lax.broadcasted_iota(jnp.int32, (CHUNK, 1), 0)
    acc = cur * w[CONV_W - 1:CONV_W, :] + b
    for j in range(1, CONV_W):
        shifted = jnp.where(row < j, pltpu.roll(prev, j, 0), pltpu.roll(cur, j, 0))
        acc = acc + shifted * w[CONV_W - 1 - j:CONV_W - j, :]
    return acc * jax.nn.sigmoid(acc)


def _mlstm_body(bias_ref, q_ref, qp_ref, k_ref, kp_ref, v_ref, o_ref, gir_ref, gfr_ref, gic_ref,
                cw_ref, cb_ref, nw_ref, y_ref, c_sc, n_sc, m_sc):
    c = pl.program_id(0)

    @pl.when(c == 0)
    def _():
        c_sc[...] = jnp.zeros_like(c_sc)
        n_sc[...] = jnp.zeros_like(n_sc)
        m_sc[...] = jnp.zeros_like(m_sc)

    first = c == 0
    nqk = NH * DQK
    cw = cw_ref[...]
    cb = cb_ref[...]
    q_all = _conv_silu(q_ref, qp_ref, cw[:, :nqk], cb[:, :nqk], first) * (DQK ** -0.5)
    k_all = _conv_silu(k_ref, kp_ref, cw[:, nqk:], cb[:, nqk:], first)

    li = lax.broadcasted_iota(jnp.int32, (CHUNK, CHUNK), 0)
    si = lax.broadcasted_iota(jnp.int32, (CHUNK, CHUNK), 1)
    causal = si <= li
    strict = jnp.where(li > si, 1.0, 0.0).astype(F32)

    for h in range(NH):
        q = q_all[:, h * DQK:(h + 1) * DQK]
        k = k_all[:, h * DQK:(h + 1) * DQK]
        v = v_ref[:, h * DV:(h + 1) * DV]
        bias_i = bias_ref[h]
        bias_f = bias_ref[NH + h]
        gi_r = gir_ref[h] + bias_i
        lf_r = _log_sigmoid(gfr_ref[h] + bias_f)
        gi_c = gic_ref[h] + bias_i

        lhs = jnp.where(causal, lf_r, 0.0)
        seg = jnp.dot(lhs, strict, preferred_element_type=F32, precision=lax.Precision.HIGHEST)
        b_c = jnp.sum(lhs, axis=-1, keepdims=True)

        m_prev = m_sc[h][:, :1]
        dmat = jnp.where(causal, seg + gi_r, -jnp.inf)
        inter = b_c + m_prev
        m_t = jnp.maximum(inter, jnp.max(dmat, axis=-1, keepdims=True))
        w_intra = jnp.exp(dmat - m_t)
        w_inter = jnp.exp(inter - m_t)

        qb = q.astype(BF16)
        kb = k.astype(BF16)
        s = lax.dot_general(qb, kb, (((1,), (1,)), ((), ())), preferred_element_type=F32) * w_intra
        c_state = c_sc[h]
        n_state = n_sc[h]
        num = jnp.dot(s.astype(BF16), v, preferred_element_type=F32)
        num = num + w_inter * jnp.dot(qb, c_state.astype(BF16), preferred_element_type=F32)
        den = jnp.sum(s, axis=-1, keepdims=True) + w_inter * jnp.sum(q * n_state, axis=-1, keepdims=True)
        hh = num * (1.0 / jnp.maximum(jnp.abs(den), jnp.exp(-m_t)))

        m_new = m_t[CHUNK - 1:CHUNK, :]
        b_last = b_c[CHUNK - 1:CHUNK, :]
        w_k = jnp.exp(b_last - b_c + gi_c - m_new)
        decay = jnp.exp(b_last + m_prev - m_new)
        kw = k * w_k
        c_sc[h] = decay * c_state + jnp.dot(kw.T.astype(BF16), v, preferred_element_type=F32)
        n_sc[h] = decay * n_state + jnp.sum(kw, axis=0, keepdims=True)
        m_sc[h] = jnp.broadcast_to(m_new, (1, CHUNK))

        gate = jax.nn.sigmoid(o_ref[:, h * DV:(h + 1) * DV].astype(F32))
        out = _head_norm(hh, nw_ref[:, h * DV:(h + 1) * DV]) * gate
        y_ref[:, h * DV:(h + 1) * DV] = out.astype(y_ref.dtype)


def _mlstm(ya, gi_row, gf_row, gi_col, if_bias, conv_w, conv_b, norm_w):
    s_len = ya.shape[0]
    nc = s_len // CHUNK
    nqk = NH * DQK
    nv = NH * DV
    prev = lambda c: jnp.maximum(c - 1, 0)
    return pl.pallas_call(
        _mlstm_body,
        grid=(nc,),
        in_specs=[
            pl.BlockSpec(memory_space=pltpu.SMEM),
            pl.BlockSpec((CHUNK, nqk), lambda c: (c, 0)),
            pl.BlockSpec((CHUNK, nqk), lambda c: (prev(c), 0)),
            pl.BlockSpec((CHUNK, nqk), lambda c: (c, 1)),
            pl.BlockSpec((CHUNK, nqk), lambda c: (prev(c), 1)),
            pl.BlockSpec((CHUNK, nv), lambda c: (c, 1)),
            pl.BlockSpec((CHUNK, nv), lambda c: (c, 2)),
            pl.BlockSpec((NH, 1, CHUNK), lambda c: (0, 0, c)),
            pl.BlockSpec((NH, 1, CHUNK), lambda c: (0, 0, c)),
            pl.BlockSpec((NH, CHUNK, 1), lambda c: (0, c, 0)),
            pl.BlockSpec((CONV_W, 2 * nqk), lambda c: (0, 0)),
            pl.BlockSpec((1, 2 * nqk), lambda c: (0, 0)),
            pl.BlockSpec((1, nv), lambda c: (0, 0)),
        ],
        out_specs=pl.BlockSpec((CHUNK, nv), lambda c: (c, 0)),
        out_shape=jax.ShapeDtypeStruct((s_len, nv), BF16),
        scratch_shapes=[
            pltpu.VMEM((NH, DQK, DV), F32),
            pltpu.VMEM((NH, 1, DQK), F32),
            pltpu.VMEM((NH, 1, CHUNK), F32),
        ],
        compiler_params=_params(("arbitrary",)),
        name="mlstm",
    )(if_bias, ya, ya, ya, ya, ya, ya, gi_row, gf_row, gi_col, conv_w, conv_b.reshape(1, -1),
      norm_w.reshape(1, -1))


def _rope(x, cos, sin):
    half = DQK // 2
    x1 = x[:, :half]
    x2 = x[:, half:]
    return jnp.concatenate([x1 * cos - x2 * sin, x2 * cos + x1 * sin], axis=-1)


def _retention_body(lg_ref, pos_ref, inv_ref, q_ref, k_ref, v_ref, g_ref, nw_ref, y_ref, r_sc):
    c = pl.program_id(0)

    @pl.when(c == 0)
    def _():
        r_sc[...] = jnp.zeros_like(r_sc)

    ang = pos_ref[...].astype(F32) * inv_ref[...]
    cos = jnp.cos(ang)
    sin = jnp.sin(ang)
    li = lax.broadcasted_iota(jnp.int32, (CHUNK, CHUNK), 0)
    si = lax.broadcasted_iota(jnp.int32, (CHUNK, CHUNK), 1)
    diff = (li - si).astype(F32)
    pos_c = lax.broadcasted_iota(jnp.int32, (CHUNK, 1), 0).astype(F32)

    for h in range(NH):
        q = _rope(q_ref[:, h * DQK:(h + 1) * DQK].astype(F32), cos, sin)
        k = _rope(k_ref[:, h * DQK:(h + 1) * DQK].astype(F32), cos, sin) * (DQK ** -0.5)
        v = v_ref[:, h * DV:(h + 1) * DV]

        lg = lg_ref[h]
        decay_mask = jnp.where(diff >= 0.0, jnp.exp(lg * jnp.maximum(diff, 0.0)), 0.0)
        q_decay = jnp.exp(lg * (pos_c + 1.0))
        k_decay = jnp.exp(lg * (CHUNK - 1.0 - pos_c))
        chunk_decay = jnp.exp(jnp.full((1, 1), CHUNK, F32) * lg)

        qb = q.astype(BF16)
        s = lax.dot_general(qb, k.astype(BF16), (((1,), (1,)), ((), ())), preferred_element_type=F32) * decay_mask
        r_state = r_sc[h]
        out = jnp.dot(s.astype(BF16), v, preferred_element_type=F32)
        out = out + jnp.dot((q * q_decay).astype(BF16), r_state.astype(BF16), preferred_element_type=F32)
        r_sc[h] = chunk_decay * r_state + jnp.dot((k * k_decay).T.astype(BF16), v, preferred_element_type=F32)

        g = g_ref[:, h * DV:(h + 1) * DV].astype(F32)
        y = _head_norm(out, nw_ref[:, h * DV:(h + 1) * DV]) * (g * jax.nn.sigmoid(g))
        y_ref[:, h * DV:(h + 1) * DV] = y.astype(y_ref.dtype)


def _retention(yb, positions_col, inv_freq, log_gamma, norm_w):
    s_len = yb.shape[0]
    nc = s_len // CHUNK
    nqk = NH * DQK
    nv = NH * DV
    return pl.pallas_call(
        _retention_body,
        grid=(nc,),
        in_specs=[
            pl.BlockSpec(memory_space=pltpu.SMEM),
            pl.BlockSpec((CHUNK, 1), lambda c: (c, 0)),
            pl.BlockSpec((1, DQK // 2), lambda c: (0, 0)),
            pl.BlockSpec((CHUNK, nqk), lambda c: (c, 0)),
            pl.BlockSpec((CHUNK, nqk), lambda c: (c, 1)),
            pl.BlockSpec((CHUNK, nv), lambda c: (c, 1)),
            pl.BlockSpec((CHUNK, nv), lambda c: (c, 2)),
            pl.BlockSpec((1, nv), lambda c: (0, 0)),
        ],
        out_specs=pl.BlockSpec((CHUNK, nv), lambda c: (c, 0)),
        out_shape=jax.ShapeDtypeStruct((s_len, nv), BF16),
        scratch_shapes=[pltpu.VMEM((NH, DQK, DV), F32)],
        compiler_params=_params(("arbitrary",)),
        name="retention",
    )(log_gamma, positions_col, inv_freq, yb, yb, yb, yb, norm_w.reshape(1, -1))


def _ffn_pre_norm(hb, resid, norm, w_gate, w_up, w_down):
    bn = 512
    steps = (D_FF_PAD // bn) * (SEQ // 1024)
    up, wd = _wres_matmul([hb], [w_gate, w_up], (0, 0), _swiglu_epilogue, n_out=D_FF_PAD, n_valid=D_FF,
                          out_dtype=BF16, bn=bn, side=(w_down, D_FF_PAD // steps, D_FF, D_FF_PAD),
                          name="swiglu_up")
    return _mm_resid(up, wd, resid, scale=0.5, norm=norm, bk=D_FF_PAD // 4)


def kernel(x, positions, ffn1_w_gate, ffn1_w_up, ffn1_w_down, ln1_w, ln1_b, w_in, if_bias, conv_w, conv_b,
           norm_m_w, norm_r_w, w_proj_m, w_proj_r, w_out, ln2_w, ln2_b, ffn2_w_gate, ffn2_w_up, ffn2_w_down,
           ln3_w, ln3_b):
    xf = x[0]
    y1 = _ffn_pre_norm(xf.astype(BF16), xf, None, ffn1_w_gate, ffn1_w_up, ffn1_w_down)
    h1b, mu1, rstd1 = _layer_norm(y1, ln1_w[0], ln1_b[0], final=False)
    norm1 = (mu1, rstd1, ln1_w[0], ln1_b[0])

    n_a = 2 * NH * DQK + 2 * NH * DV
    n_b = 2 * NH * DQK + 2 * NH * DV + 2 * D_MODEL
    w_in_t = jnp.swapaxes(w_in, 1, 2)
    ya = _wres_matmul([h1b], [w_in_t], (0,), _plain_epilogue, n_out=n_a, out_dtype=BF16, bn=1024,
                      transposed=True, name="in_proj_a")
    yb = _wres_matmul([h1b], [w_in_t], (0,), _plain_epilogue, n_out=n_b, out_dtype=BF16, bn=1024,
                      transposed=True, col0=n_a, shift=2 * NH, name="in_proj_b")
    yif = _gates_matmul(h1b, w_in_t, col0=n_a)

    g_rows = yif.T
    gi_row = g_rows[:NH].reshape(NH, 1, SEQ)
    gf_row = g_rows[NH:].reshape(NH, 1, SEQ)
    gi_col = g_rows[:NH].reshape(NH, SEQ, 1)
    y_m = _mlstm(ya, gi_row, gf_row, gi_col, if_bias[0], conv_w[0], conv_b[0], norm_m_w[0])

    inv_freq = (ROPE_BASE ** (-jnp.arange(0, DQK, 2, dtype=F32) / DQK)).reshape(1, DQK // 2)
    log_gamma = jnp.log(1.0 - 2.0 ** (-5.0 - jnp.arange(NH, dtype=F32)))
    y_r = _retention(yb, positions.reshape(SEQ, 1), inv_freq, log_gamma, norm_r_w[0])

    gate_col0 = 2 * NH * DQK + 2 * NH * DV
    bn = 512
    steps = (D_MODEL // bn) * (SEQ // 1024)
    merged, wo = _wres_matmul([y_m, y_r], [w_proj_m, w_proj_r], (0, 1), _merge_epilogue, n_out=D_MODEL,
                              out_dtype=BF16, bn=bn, extras=((yb, gate_col0), (yb, gate_col0 + D_MODEL)),
                              side=(w_out, D_MODEL // steps, D_MODEL, D_MODEL), name="merge")
    y2 = _mm_resid(merged, wo, y1, scale=1.0, norm=norm1, bk=D_MODEL // 2)
    h2b, mu2, rstd2 = _layer_norm(y2, ln2_w[0], ln2_b[0], final=False)
    norm2 = (mu2, rstd2, ln2_w[0], ln2_b[0])

    y3 = _ffn_pre_norm(h2b, y2, norm2, ffn2_w_gate, ffn2_w_up, ffn2_w_down)
    (out,) = _layer_norm(y3, ln3_w[0], ln3_b[0], final=True)
    return out.reshape(1, SEQ, D_MODEL)
```

```python
import functools
import math

import jax
import jax.numpy as jnp
from jax import lax
from jax.experimental import pallas as pl
from jax.experimental.pallas import tpu as pltpu

D_MODEL = 4096
SEQ = 8192
NH = 4
DQK = 256
DV = 512
CONV_W = 4
D_FF = 11008
D_FF_PAD = 11264
CHUNK = 128
LN_EPS = 1e-5
HEAD_EPS = 1e-6
ALPHA = 2.0 ** 0.25
ROPE_BASE = 10000.0

F32 = jnp.float32
BF16 = jnp.bfloat16

VMEM_LIMIT_BYTES = 56 * 1024 * 1024


def _params(sem):
    return pltpu.CompilerParams(dimension_semantics=sem, vmem_limit_bytes=VMEM_LIMIT_BYTES)


def _wres_body(*refs, n_x, pairs, transposed, shift, n_extra, epilogue, ck, nb, ms, side, rsplit, last_cols):
    n_w = len(pairs)
    pos = 0
    x_refs = refs[pos:pos + n_x]; pos += n_x
    wmain = refs[pos:pos + n_w]; pos += n_w
    wext = refs[pos:pos + n_w] if shift else ()
    pos += n_w if shift else 0
    extra = refs[pos:pos + n_extra]; pos += n_extra
    side_in = refs[pos] if side else None
    pos += 1 if side else 0
    out_ref = refs[pos]; pos += 1
    side_out = refs[pos] if side else None
    pos += 1 if side else 0
    w_sc = [refs[pos + 2 * w:pos + 2 * w + 2] for w in range(n_w)]

    j = pl.program_id(0)
    i = pl.program_id(1)
    row0 = pl.multiple_of(i * ck, ck)

    def cast_piece(cast_slot, p, n_pieces):
        cp = ck // n_pieces
        for w in range(n_w):
            if transposed:
                blk = wmain[w][:, p * cp:(p + 1) * cp]
                if shift:
                    blk = jnp.concatenate([blk[shift:, :], wext[w][:, p * cp:(p + 1) * cp]], axis=0)
                chunk = blk.T
            else:
                chunk = wmain[w][p * cp:(p + 1) * cp, :]
            w_sc[w][cast_slot][pl.ds(pl.multiple_of(row0 + p * cp, cp), cp), :] = chunk.astype(BF16)
        if side:
            s_rows, s_valid = side
            sp = s_rows // n_pieces
            blk = jnp.minimum(j * ms + i, nb * ms - 1)
            ridx = blk * s_rows + p * sp + lax.broadcasted_iota(jnp.int32, (sp, 1), 0)
            piece = side_in[p * sp:(p + 1) * sp, :]
            side_out[p * sp:(p + 1) * sp, :] = jnp.where(ridx < s_valid, piece, 0.0).astype(BF16)

    def step(cast_slot, read_slot, ncols=None):
        if read_slot is None:
            cast_piece(cast_slot, 0, 1)
            return
        bn = out_ref.shape[1]
        rb = out_ref.shape[0] // rsplit
        n_pieces = max(1, rsplit // 2)
        for r in range(rsplit):
            rows = slice(r * rb, (r + 1) * rb)
            if ncols is None:
                dots = [jnp.dot(x_refs[xi][rows, :], w_sc[w][read_slot][...], preferred_element_type=F32)
                        for w, xi in enumerate(pairs)]
                res = epilogue(dots, [e[rows, :] for e in extra], j - 1)
            else:
                dots = [jnp.dot(x_refs[xi][rows, :], w_sc[w][read_slot][:, :ncols], preferred_element_type=F32)
                        for w, xi in enumerate(pairs)]
                res = epilogue(dots, [e[rows, :ncols] for e in extra], j - 1)
                out_ref[rows, ncols:] = jnp.zeros((rb, bn - ncols), out_ref.dtype)
            out_ref[rows, :res.shape[1]] = res.astype(out_ref.dtype)
            if r < n_pieces:
                cast_piece(cast_slot, r, n_pieces)

    even = j % 2 == 0
    full = j < nb if last_cols else j <= nb
    pl.when(j == 0)(lambda: step(0, None))
    pl.when(jnp.logical_and(jnp.logical_and(j >= 1, even), full))(lambda: step(0, 1))
    pl.when(jnp.logical_and(jnp.logical_not(even), full))(lambda: step(1, 0))
    if last_cols:
        pl.when(j == nb)(lambda: step(nb % 2, 1 - nb % 2, last_cols))


def _wres_matmul(xs, ws, pairs, epilogue, *, n_out, out_dtype, bn, n_valid=None, transposed=False, col0=0,
                 shift=0, extras=(), side=None, bm=1024, rsplit=8, name):
    m, k = xs[0].shape
    ms = m // bm
    ck = k // ms
    nb = pl.cdiv(n_out, bn)
    cb0 = col0 // bn
    assert not shift or transposed
    n_valid = n_out if n_valid is None else n_valid
    last_cols = n_valid - (nb - 1) * bn if n_valid < nb * bn else None

    def row_blk(j, i):
        return jnp.where(j > 0, i, 0)

    def w_blk(j):
        return cb0 + jnp.minimum(j, nb - 1)

    in_specs = [pl.BlockSpec((bm, k), lambda j, i: (row_blk(j, i), 0)) for _ in xs]
    if transposed:
        in_specs += [pl.BlockSpec((None, bn, ck), lambda j, i: (0, w_blk(j), i)) for _ in ws]
    else:
        in_specs += [pl.BlockSpec((None, ck, bn), lambda j, i: (0, i, w_blk(j))) for _ in ws]
    args = list(xs) + list(ws)
    if shift:
        eb = bn // shift
        in_specs += [pl.BlockSpec((None, shift, ck), lambda j, i: (0, (w_blk(j) + 1) * eb, i)) for _ in ws]
        args += list(ws)
    for arr, c0 in extras:
        in_specs.append(pl.BlockSpec((bm, bn), lambda j, i, b0=c0 // bn: (row_blk(j, i), b0 + jnp.maximum(j - 1, 0))))
        args.append(arr)
    out_specs = [pl.BlockSpec((bm, bn), lambda j, i: (row_blk(j, i), jnp.maximum(j - 1, 0)))]
    out_shape = [jax.ShapeDtypeStruct((m, n_out), out_dtype)]
    body_side = None
    if side is not None:
        s_arr, s_rows, s_valid, s_out_rows = side
        assert s_out_rows == nb * ms * s_rows
        last_in = s_valid // s_rows - 1
        s_cols = s_arr.shape[2]

        def side_blk(j, i):
            return jnp.minimum(j * ms + i, nb * ms - 1)

        in_specs.append(pl.BlockSpec((None, s_rows, s_cols),
                                     lambda j, i: (0, jnp.minimum(side_blk(j, i), last_in), 0)))
        args.append(s_arr)
        out_specs.append(pl.BlockSpec((s_rows, s_cols), lambda j, i: (side_blk(j, i), 0)))
        out_shape.append(jax.ShapeDtypeStruct((s_out_rows, s_cols), BF16))
        body_side = (s_rows, s_valid)
    body = functools.partial(_wres_body, n_x=len(xs), pairs=tuple(pairs), transposed=transposed, shift=shift,
                             n_extra=len(extras), last_cols=last_cols,
                             epilogue=epilogue, ck=ck, nb=nb, ms=ms, side=body_side, rsplit=rsplit)
    res = pl.pallas_call(
        body,
        grid=(nb + 1, ms),
        in_specs=in_specs,
        out_specs=out_specs,
        out_shape=out_shape,
        scratch_shapes=[pltpu.VMEM((k, bn), BF16) for _ in range(2 * len(ws))],
        compiler_params=_params(("arbitrary", "arbitrary")),
        name=name,
    )(*args)
    return res if side is not None else res[0]


def _swiglu_epilogue(dots, extras, jb):
    g, u = dots
    return g * jax.nn.sigmoid(g) * u


def _plain_epilogue(dots, extras, jb):
    return dots[0]


def _merge_epilogue(dots, extras, jb):
    pm, pr = dots
    ga, gb = extras
    return jax.nn.sigmoid(ga.astype(F32)) * pm + jax.nn.sigmoid(gb.astype(F32)) * pr


def _mm_resid_body(x_ref, w_ref, r_ref, *rest, scale, nk, rsplit, normed):
    if normed:
        mu_ref, rstd_ref, lw_ref, lb_ref, o_ref = rest
    else:
        (o_ref,) = rest
    kk = pl.program_id(2)
    rb = o_ref.shape[0] // rsplit

    def run(first, last):
        for r in range(rsplit):
            rows = slice(r * rb, (r + 1) * rb)
            d = jnp.dot(x_ref[rows, :], w_ref[...], preferred_element_type=F32)
            if not first:
                d = o_ref[rows, :] + d
            if last:
                res = r_ref[rows, :]
                if normed:
                    res = (res - mu_ref[rows, :]) * rstd_ref[rows, :] * lw_ref[...] + lb_ref[...]
                d = ALPHA * res + scale * d
            o_ref[rows, :] = d

    pl.when(kk == 0)(lambda: run(True, nk == 1))
    if nk > 2:
        pl.when(jnp.logical_and(kk > 0, kk < nk - 1))(lambda: run(False, False))
    if nk > 1:
        pl.when(kk == nk - 1)(lambda: run(False, True))


def _mm_resid(xb, w, resid, *, scale, bk, norm=None, bm=1024, bn=1024, rsplit=4):
    m, k = xb.shape
    n = w.shape[1]
    nk = k // bk
    in_specs = [
        pl.BlockSpec((bm, bk), lambda i, j, kk: (i, kk)),
        pl.BlockSpec((bk, bn), lambda i, j, kk: (kk, j)),
        pl.BlockSpec((bm, bn), lambda i, j, kk: (i, j)),
    ]
    args = [xb, w, resid]
    if norm is not None:
        mu, rstd, lw, lb = norm
        in_specs += [
            pl.BlockSpec((bm, 1), lambda i, j, kk: (i, 0)),
            pl.BlockSpec((bm, 1), lambda i, j, kk: (i, 0)),
            pl.BlockSpec((1, bn), lambda i, j, kk: (0, j)),
            pl.BlockSpec((1, bn), lambda i, j, kk: (0, j)),
        ]
        args += [mu, rstd, lw.reshape(1, n), lb.reshape(1, n)]
    return pl.pallas_call(
        functools.partial(_mm_resid_body, scale=scale, nk=nk, rsplit=rsplit, normed=norm is not None),
        grid=(m // bm, n // bn, nk),
        in_specs=in_specs,
        out_specs=pl.BlockSpec((bm, bn), lambda i, j, kk: (i, j)),
        out_shape=jax.ShapeDtypeStruct((m, n), F32),
        compiler_params=_params(("parallel", "parallel", "arbitrary")),
        name="mm_resid",
    )(*args)


def _gates_body(x_ref, wt_ref, o_ref):
    o_ref[...] = lax.dot_general(x_ref[...], wt_ref[...].astype(BF16), (((1,), (1,)), ((), ())),
                                 preferred_element_type=F32)


def _gates_matmul(xb, wt, *, col0, n_cols=8, bm=1024):
    m, k = xb.shape
    return pl.pallas_call(
        _gates_body,
        grid=(m // bm,),
        in_specs=[
            pl.BlockSpec((bm, k), lambda i: (i, 0)),
            pl.BlockSpec((None, n_cols, k), lambda i: (0, col0 // n_cols, 0)),
        ],
        out_specs=pl.BlockSpec((bm, n_cols), lambda i: (i, 0)),
        out_shape=jax.ShapeDtypeStruct((m, n_cols), F32),
        compiler_params=_params(("parallel",)),
        name="gates_mm",
    )(xb, wt)


LN_SUB_ROWS = 16


def _ln_body(y_ref, w_ref, b_ref, o_ref, *stat_refs):
    def sub(r, carry):
        rows = pl.ds(pl.multiple_of(r * LN_SUB_ROWS, LN_SUB_ROWS), LN_SUB_ROWS)
        y = y_ref[rows, :]
        mu = jnp.mean(y, axis=-1, keepdims=True)
        d = y - mu
        var = jnp.mean(d * d, axis=-1, keepdims=True)
        rstd = lax.rsqrt(var + LN_EPS)
        out = d * rstd * w_ref[...] + b_ref[...]
        o_ref[rows, :] = out.astype(o_ref.dtype)
        if stat_refs:
            mu_ref, rstd_ref = stat_refs
            mu_ref[rows, :] = mu
            rstd_ref[rows, :] = rstd
        return carry

    lax.fori_loop(0, y_ref.shape[0] // LN_SUB_ROWS, sub, 0, unroll=8)


def _layer_norm(y, w, b, *, final, bm=512):
    m, d = y.shape
    row = pl.BlockSpec((bm, d), lambda i: (i, 0))
    stat = pl.BlockSpec((bm, 1), lambda i: (i, 0))
    if final:
        out_specs = [row]
        out_shape = [jax.ShapeDtypeStruct((m, d), F32)]
    else:
        out_specs = [row, stat, stat]
        out_shape = [jax.ShapeDtypeStruct((m, d), BF16), jax.ShapeDtypeStruct((m, 1), F32),
                     jax.ShapeDtypeStruct((m, 1), F32)]
    return pl.pallas_call(
        _ln_body,
        grid=(m // bm,),
        in_specs=[row, pl.BlockSpec((1, d), lambda i: (0, 0)), pl.BlockSpec((1, d), lambda i: (0, 0))],
        out_specs=out_specs,
        out_shape=out_shape,
        compiler_params=_params(("parallel",)),
        name="layer_norm",
    )(y, w.reshape(1, d), b.reshape(1, d))


def _head_norm(hh, w):
    mu = jnp.mean(hh, axis=-1, keepdims=True)
    d = hh - mu
    var = jnp.mean(d * d, axis=-1, keepdims=True)
    return d * lax.rsqrt(var + HEAD_EPS) * w


def _log_sigmoid(x):
    return jnp.minimum(x, 0.0) - jnp.log1p(jnp.exp(-jnp.abs(x)))


def _conv_silu(cur_ref, prev_ref, w, b, first_chunk):
    cur = cur_ref[...].astype(F32)
    prev = jnp.where(first_chunk, 0.0, prev_ref[...].astype(F32))
    row = lax.broadcasted_iota(jnp.int32, (CHUNK, 1), 0)
    acc = cur * w[CONV_W - 1:CONV_W, :] + b
    for j in range(1, CONV_W):
        shifted = jnp.where(row < j, pltpu.roll(prev, j, 0), pltpu.roll(cur, j, 0))
        acc = acc + shifted * w[CONV_W - 1 - j:CONV_W - j, :]
    return acc * jax.nn.sigmoid(acc)


def _mlstm_heads(bias_ref, q_ref, qp_ref, k_ref, kp_ref, v_ref, o_ref, gir_ref, gfr_ref, gic_ref,
                 cw_ref, cb_ref, nw_ref, y_ref, c_sc, n_sc, m_sc):
    c = pl.program_id(0)

    @pl.when(c == 0)
    def _():
        c_sc[...] = jnp.zeros_like(c_sc)
        n_sc[...] = jnp.zeros_like(n_sc)
        m_sc[...] = jnp.zeros_like(m_sc)

    first = c == 0
    nqk = NH * DQK
    cw = cw_ref[...]
    cb = cb_ref[...]
    q_all = _conv_silu(q_ref, qp_ref, cw[:, :nqk], cb[:, :nqk], first) * (DQK ** -0.5)
    k_all = _conv_silu(k_ref, kp_ref, cw[:, nqk:], cb[:, nqk:], first)

    li = lax.broadcasted_iota(jnp.int32, (CHUNK, CHUNK), 0)
    si = lax.broadcasted_iota(jnp.int32, (CHUNK, CHUNK), 1)
    causal = si <= li
    strict = jnp.where(li > si, 1.0, 0.0).astype(F32)

    def head(h):
        q = q_all[:, h * DQK:(h + 1) * DQK]
        k = k_all[:, h * DQK:(h + 1) * DQK]
        v = v_ref[:, h * DV:(h + 1) * DV]
        bias_i = bias_ref[h]
        bias_f = bias_ref[NH + h]
        gi_r = gir_ref[h] + bias_i
        lf_r = _log_sigmoid(gfr_ref[h] + bias_f)
        gi_c = gic_ref[h] + bias_i

        lhs = jnp.where(causal, lf_r, 0.0)
        seg = jnp.dot(lhs, strict, preferred_element_type=F32, precision=lax.Precision.HIGHEST)
        b_c = jnp.sum(lhs, axis=-1, keepdims=True)

        m_prev = m_sc[h][:, :1]
        dmat = jnp.where(causal, seg + gi_r, -jnp.inf)
        inter = b_c + m_prev
        m_t = jnp.maximum(inter, jnp.max(dmat, axis=-1, keepdims=True))
        w_intra = jnp.exp(dmat - m_t)
        w_inter = jnp.exp(inter - m_t)

        qb = q.astype(BF16)
        kb = k.astype(BF16)
        s = lax.dot_general(qb, kb, (((1,), (1,)), ((), ())), preferred_element_type=F32) * w_intra
        c_state = c_sc[h]
        n_state = n_sc[h]
        num = jnp.dot(s.astype(BF16), v, preferred_element_type=F32)
        num = num + w_inter * jnp.dot(qb, c_state.astype(BF16), preferred_element_type=F32)
        den = jnp.sum(s, axis=-1, keepdims=True) + w_inter * jnp.sum(q * n_state, axis=-1, keepdims=True)
        hh = num * (1.0 / jnp.maximum(jnp.abs(den), jnp.exp(-m_t)))

        m_new = m_t[CHUNK - 1:CHUNK, :]
        b_last = b_c[CHUNK - 1:CHUNK, :]
        w_k = jnp.exp(b_last - b_c + gi_c - m_new)
        decay = jnp.exp(b_last + m_prev - m_new)
        kw = k * w_k
        c_sc[h] = decay * c_state + jnp.dot(kw.T.astype(BF16), v, preferred_element_type=F32)
        n_sc[h] = decay * n_state + jnp.sum(kw, axis=0, keepdims=True)
        m_sc[h] = jnp.broadcast_to(m_new, (1, CHUNK))

        gate = jax.nn.sigmoid(o_ref[:, h * DV:(h + 1) * DV].astype(F32))
        out = _head_norm(hh, nw_ref[:, h * DV:(h + 1) * DV]) * gate
        y_ref[:, h * DV:(h + 1) * DV] = out.astype(y_ref.dtype)

    return head


def _rope(x, cos, sin):
    half = DQK // 2
    x1 = x[:, :half]
    x2 = x[:, half:]
    return jnp.concatenate([x1 * cos - x2 * sin, x2 * cos + x1 * sin], axis=-1)


def _retention_heads(lg_ref, pos_ref, inv_ref, q_ref, k_ref, v_ref, g_ref, nw_ref, y_ref, r_sc):
    c = pl.program_id(0)

    @pl.when(c == 0)
    def _():
        r_sc[...] = jnp.zeros_like(r_sc)

    ang = pos_ref[...].astype(F32) * inv_ref[...]
    cos = jnp.cos(ang)
    sin = jnp.sin(ang)
    li = lax.broadcasted_iota(jnp.int32, (CHUNK, CHUNK), 0)
    si = lax.broadcasted_iota(jnp.int32, (CHUNK, CHUNK), 1)
    diff = (li - si).astype(F32)
    pos_c = lax.broadcasted_iota(jnp.int32, (CHUNK, 1), 0).astype(F32)

    def head(h):
        q = _rope(q_ref[:, h * DQK:(h + 1) * DQK].astype(F32), cos, sin)
        k = _rope(k_ref[:, h * DQK:(h + 1) * DQK].astype(F32), cos, sin) * (DQK ** -0.5)
        v = v_ref[:, h * DV:(h + 1) * DV]

        lg = lg_ref[h]
        decay_mask = jnp.where(diff >= 0.0, jnp.exp(lg * jnp.maximum(diff, 0.0)), 0.0)
        q_decay = jnp.exp(lg * (pos_c + 1.0))
        k_decay = jnp.exp(lg * (CHUNK - 1.0 - pos_c))
        chunk_decay = jnp.exp(jnp.full((1, 1), CHUNK, F32) * lg)

        qb = q.astype(BF16)
        s = lax.dot_general(qb, k.astype(BF16), (((1,), (1,)), ((), ())), preferred_element_type=F32) * decay_mask
        r_state = r_sc[h]
        out = jnp.dot(s.astype(BF16), v, preferred_element_type=F32)
        out = out + jnp.dot((q * q_decay).astype(BF16), r_state.astype(BF16), preferred_element_type=F32)
        r_sc[h] = chunk_decay * r_state + jnp.dot((k * k_decay).T.astype(BF16), v, preferred_element_type=F32)

        g = g_ref[:, h * DV:(h + 1) * DV].astype(F32)
        y = _head_norm(out, nw_ref[:, h * DV:(h + 1) * DV]) * (g * jax.nn.sigmoid(g))
        y_ref[:, h * DV:(h + 1) * DV] = y.astype(y_ref.dtype)

    return head


N_MLSTM_IN = 13
N_RET_IN = 8


def _mixers_body(*refs):
    m_in = refs[:N_MLSTM_IN]
    r_in = refs[N_MLSTM_IN:N_MLSTM_IN + N_RET_IN]
    ym_ref, yr_ref, c_sc, n_sc, m_sc, r_sc = refs[N_MLSTM_IN + N_RET_IN:]
    mlstm_head = _mlstm_heads(*m_in, ym_ref, c_sc, n_sc, m_sc)
    ret_head = _retention_heads(*r_in, yr_ref, r_sc)
    for h in range(NH):
        mlstm_head(h)
        ret_head(h)


def _mixers(ya, yb, gi_row, gf_row, gi_col, if_bias, conv_w, conv_b, norm_m_w, positions_col, inv_freq,
            log_gamma, norm_r_w):
    s_len = ya.shape[0]
    nc = s_len // CHUNK
    nqk = NH * DQK
    nv = NH * DV
    prev = lambda c: jnp.maximum(c - 1, 0)
    in_specs = [
        pl.BlockSpec(memory_space=pltpu.SMEM),
        pl.BlockSpec((CHUNK, nqk), lambda c: (c, 0)),
        pl.BlockSpec((CHUNK, nqk), lambda c: (prev(c), 0)),
        pl.BlockSpec((CHUNK, nqk), lambda c: (c, 1)),
        pl.BlockSpec((CHUNK, nqk), lambda c: (prev(c), 1)),
        pl.BlockSpec((CHUNK, nv), lambda c: (c, 1)),
        pl.BlockSpec((CHUNK, nv), lambda c: (c, 2)),
        pl.BlockSpec((NH, 1, CHUNK), lambda c: (0, 0, c)),
        pl.BlockSpec((NH, 1, CHUNK), lambda c: (0, 0, c)),
        pl.BlockSpec((NH, CHUNK, 1), lambda c: (0, c, 0)),
        pl.BlockSpec((CONV_W, 2 * nqk), lambda c: (0, 0)),
        pl.BlockSpec((1, 2 * nqk), lambda c: (0, 0)),
        pl.BlockSpec((1, nv), lambda c: (0, 0)),
        pl.BlockSpec(memory_space=pltpu.SMEM),
        pl.BlockSpec((CHUNK, 1), lambda c: (c, 0)),
        pl.BlockSpec((1, DQK // 2), lambda c: (0, 0)),
        pl.BlockSpec((CHUNK, nqk), lambda c: (c, 0)),
        pl.BlockSpec((CHUNK, nqk), lambda c: (c, 1)),
        pl.BlockSpec((CHUNK, nv), lambda c: (c, 1)),
        pl.BlockSpec((CHUNK, nv), lambda c: (c, 2)),
        pl.BlockSpec((1, nv), lambda c: (0, 0)),
    ]
    assert len(in_specs) == N_MLSTM_IN + N_RET_IN
    out_spec = pl.BlockSpec((CHUNK, nv), lambda c: (c, 0))
    return pl.pallas_call(
        _mixers_body,
        grid=(nc,),
        in_specs=in_specs,
        out_specs=[out_spec, out_spec],
        out_shape=[jax.ShapeDtypeStruct((s_len, nv), BF16), jax.ShapeDtypeStruct((s_len, nv), BF16)],
        scratch_shapes=[
            pltpu.VMEM((NH, DQK, DV), F32),
            pltpu.VMEM((NH, 1, DQK), F32),
            pltpu.VMEM((NH, 1, CHUNK), F32),
            pltpu.VMEM((NH, DQK, DV), F32),
        ],
        compiler_params=_params(("arbitrary",)),
        name="mixers",
    )(if_bias, ya, ya, ya, ya, ya, ya, gi_row, gf_row, gi_col, conv_w, conv_b.reshape(1, -1),
      norm_m_w.reshape(1, -1),
      log_gamma, positions_col, inv_freq, yb, yb, yb, yb, norm_r_w.reshape(1, -1))


def _ffn_pre_norm(hb, resid, norm, w_gate, w_up, w_down):
    bn = 512
    steps = (D_FF_PAD // bn) * (SEQ // 1024)
    up, wd = _wres_matmul([hb], [w_gate, w_up], (0, 0), _swiglu_epilogue, n_out=D_FF_PAD, n_valid=D_FF,
                          out_dtype=BF16, bn=bn, side=(w_down, D_FF_PAD // steps, D_FF, D_FF_PAD),
                          name="swiglu_up")
    return _mm_resid(up, wd, resid, scale=0.5, norm=norm, bk=D_FF_PAD // 4)


def kernel(x, positions, ffn1_w_gate, ffn1_w_up, ffn1_w_down, ln1_w, ln1_b, w_in, if_bias, conv_w, conv_b,
           norm_m_w, norm_r_w, w_proj_m, w_proj_r, w_out, ln2_w, ln2_b, ffn2_w_gate, ffn2_w_up, ffn2_w_down,
           ln3_w, ln3_b):
    xf = x[0]
    y1 = _ffn_pre_norm(xf.astype(BF16), xf, None, ffn1_w_gate, ffn1_w_up, ffn1_w_down)
    h1b, mu1, rstd1 = _layer_norm(y1, ln1_w[0], ln1_b[0], final=False)
    norm1 = (mu1, rstd1, ln1_w[0], ln1_b[0])

    n_a = 2 * NH * DQK + 2 * NH * DV
    n_b = 2 * NH * DQK + 2 * NH * DV + 2 * D_MODEL
    w_in_t = jnp.swapaxes(w_in, 1, 2)
    ya = _wres_matmul([h1b], [w_in_t], (0,), _plain_epilogue, n_out=n_a, out_dtype=BF16, bn=1024,
                      transposed=True, name="in_proj_a")
    yb = _wres_matmul([h1b], [w_in_t], (0,), _plain_epilogue, n_out=n_b, out_dtype=BF16, bn=1024,
                      transposed=True, col0=n_a, shift=2 * NH, name="in_proj_b")
    yif = _gates_matmul(h1b, w_in_t, col0=n_a)

    g_rows = yif.T
    gi_row = g_rows[:NH].reshape(NH, 1, SEQ)
    gf_row = g_rows[NH:].reshape(NH, 1, SEQ)
    gi_col = g_rows[:NH].reshape(NH, SEQ, 1)
    inv_freq = (ROPE_BASE ** (-jnp.arange(0, DQK, 2, dtype=F32) / DQK)).reshape(1, DQK // 2)
    log_gamma = jnp.log(1.0 - 2.0 ** (-5.0 - jnp.arange(NH, dtype=F32)))
    y_m, y_r = _mixers(ya, yb, gi_row, gf_row, gi_col, if_bias[0], conv_w[0], conv_b[0], norm_m_w[0],
                       positions.reshape(SEQ, 1), inv_freq, log_gamma, norm_r_w[0])

    gate_col0 = 2 * NH * DQK + 2 * NH * DV
    bn = 512
    steps = (D_MODEL // bn) * (SEQ // 1024)
    merged, wo = _wres_matmul([y_m, y_r], [w_proj_m, w_proj_r], (0, 1), _merge_epilogue, n_out=D_MODEL,
                              out_dtype=BF16, bn=bn, extras=((yb, gate_col0), (yb, gate_col0 + D_MODEL)),
                              side=(w_out, D_MODEL // steps, D_MODEL, D_MODEL), name="merge")
    y2 = _mm_resid(merged, wo, y1, scale=1.0, norm=norm1, bk=D_MODEL // 2)
    h2b, mu2, rstd2 = _layer_norm(y2, ln2_w[0], ln2_b[0], final=False)
    norm2 = (mu2, rstd2, ln2_w[0], ln2_b[0])

    y3 = _ffn_pre_norm(h2b, y2, norm2, ffn2_w_gate, ffn2_w_up, ffn2_w_down)
    (out,) = _layer_norm(y3, ln3_w[0], ln3_b[0], final=True)
    return out.reshape(1, SEQ, D_MODEL)
```
